```python
import math
import jax, jax.numpy as jnp
from jax import lax
import numpy as np

D_MODEL = 1024
BATCH = 4
SEQ = 4096
DEPTH = 1
DEC_BATCH = 32
DEC_SEQ = 1
PAST_LEN = 16384
PAGE_SIZE = 128

N_META = 16
M_HEADS = 4
M_DK = 128
M_DV = 128
M_WIDTH = M_HEADS * M_DV
QK_M = 2 * M_HEADS * M_DK
CONV_W = 4
CHUNK = 64
FGATE_BIAS_LO = 3.0
FGATE_BIAS_HI = 6.0
NEG_BIG = -1e30
A_HEADS = 8
A_DH = 64
A_DV = 2 * A_DH
A_QK = A_HEADS * 2 * A_DH
A_WIDTH = A_HEADS * A_DV
ROT_DIM = A_DH // 4
ROPE_THETA = 500000.0
Q_BLOCK = 128
D_FF = 4 * D_MODEL
EPS = 1e-6
SPLIT_SIZES = (QK_M, M_WIDTH, M_WIDTH, 2 * M_HEADS, A_QK, A_QK, A_WIDTH, D_MODEL, D_MODEL)
D_IN = QK_M + 2 * M_WIDTH + 2 * M_HEADS + 2 * A_QK + A_WIDTH + 2 * D_MODEL

kernel_name = 'hybrid_mlstm_diffattn_step'


def rmsnorm(x, g):
    xf = x.astype(jnp.float32)
    y = xf * lax.rsqrt(jnp.mean(xf * xf, axis=-1, keepdims=True) + EPS)
    return (y * g.astype(jnp.float32)).astype(x.dtype)


def head_rms(h):
    return h * lax.rsqrt(jnp.mean(h * h, axis=-1, keepdims=True) + EPS)


def split_cols(z):
    offs, acc = [], 0
    for s in SPLIT_SIZES[:-1]:
        acc += s
        offs.append(acc)
    return jnp.split(z, offs, axis=-1)


def rope_partial(x, pos):
    inv = ROPE_THETA ** (-jnp.arange(0, ROT_DIM, 2, dtype=jnp.float32) / ROT_DIM)
    ang = pos.astype(jnp.float32)[:, None] * inv[None, :]
    cos = jnp.cos(ang)[:, None, None, :]
    sin = jnp.sin(ang)[:, None, None, :]
    xf = x.astype(jnp.float32)
    half = ROT_DIM // 2
    x1, x2 = xf[..., :half], xf[..., half:ROT_DIM]
    out = jnp.concatenate([x1 * cos - x2 * sin, x2 * cos + x1 * sin, xf[..., ROT_DIM:]], axis=-1)
    return out.astype(x.dtype)


def causal_conv_silu(u, buf, w, b):
    t = u.shape[1]
    up = jnp.concatenate([buf.astype(u.dtype), u], axis=1)
    y = up[:, 0:t] * w[0] + b
    for j in range(1, CONV_W):
        y = y + up[:, j:j + t] * w[j]
    return jax.nn.silu(y), up[:, t:]


def branch_inputs(xn, pos, conv_buf, w_in, b_if, conv_w, conv_b):
    bsz, t = xn.shape[:2]
    qk_pre, m_v, m_o, m_if, a_q, a_k, a_v, g_a, g_b = split_cols(xn @ w_in)
    qk, new_buf = causal_conv_silu(qk_pre, conv_buf, conv_w, conv_b)
    m_q = qk[..., :QK_M // 2].reshape(bsz, t, M_HEADS, M_DK).astype(jnp.float32)
    m_k = qk[..., QK_M // 2:].reshape(bsz, t, M_HEADS, M_DK).astype(jnp.float32) * (M_DK ** -0.5)
    m_v = m_v.reshape(bsz, t, M_HEADS, M_DV).astype(jnp.float32)
    gates = m_if.astype(jnp.float32) + b_if.astype(jnp.float32)
    m_li = gates[..., :M_HEADS]
    m_lf = jax.nn.log_sigmoid(gates[..., M_HEADS:])
    a_q = rope_partial(a_q.reshape(bsz, t, A_HEADS, 2, A_DH), pos)
    a_k = rope_partial(a_k.reshape(bsz, t, A_HEADS, 2, A_DH), pos)
    a_v = a_v.reshape(bsz, t, A_HEADS, A_DV)
    return (m_q, m_k, m_v, m_li, m_lf, m_o, a_q, a_k, a_v, g_a, g_b, new_buf)


def mlstm_chunk(state, chunk):
    c, n, m = state
    q, k, v, li, lf = chunk
    L = q.shape[1]
    b = jnp.cumsum(lf, axis=1)
    log_d = b[:, :, None, :] - b[:, None, :, :] + li[:, None, :, :]
    mask = jnp.tril(jnp.ones((L, L), dtype=bool))[None, :, :, None]
    log_d = jnp.where(mask, log_d, -jnp.inf)
    inter = b + m[:, None, :]
    m_t = jnp.maximum(inter, jnp.max(log_d, axis=2))
    d_w = jnp.exp(log_d - m_t[:, :, None, :])
    w_inter = jnp.exp(inter - m_t)
    s = jnp.einsum('bthd,bshd->btsh', q, k) * d_w
    num = jnp.einsum('btsh,bshe->bthe', s, v) + w_inter[..., None] * jnp.einsum('bhed,bthd->bthe', c, q)
    den = jnp.sum(s, axis=2) + w_inter * jnp.einsum('bhd,bthd->bth', n, q)
    h = num / jnp.maximum(jnp.abs(den), jnp.exp(-m_t))[..., None]
    m_new = m_t[:, -1]
    w_last = jnp.exp(b[:, -1:] - b + li - m_new[:, None])
    decay = jnp.exp(b[:, -1] + m - m_new)
    c_new = decay[..., None, None] * c + jnp.einsum('bsh,bshe,bshd->bhed', w_last, v, k)
    n_new = decay[..., None] * n + jnp.einsum('bsh,bshd->bhd', w_last, k)
    return (c_new, n_new, m_new), h


def mlstm_prompt(q, k, v, li, lf):
    bsz, t = q.shape[:2]
    pad = (-t) % CHUNK

    def front(a, val):
        return jnp.concatenate([jnp.full((bsz, pad) + a.shape[2:], val, a.dtype), a], axis=1)

    q, k, v = front(q, 0.0), front(k, 0.0), front(v, 0.0)
    li, lf = front(li, NEG_BIG), front(lf, 0.0)
    nc = (t + pad) // CHUNK

    def to_chunks(a):
        return a.reshape((bsz, nc, CHUNK) + a.shape[2:]).swapaxes(0, 1)

    init = (jnp.zeros((bsz, M_HEADS, M_DV, M_DK), jnp.float32),
            jnp.zeros((bsz, M_HEADS, M_DK), jnp.float32),
            jnp.zeros((bsz, M_HEADS), jnp.float32))
    (c, n, m), h = lax.scan(mlstm_chunk, init, (to_chunks(q), to_chunks(k), to_chunks(v), to_chunks(li), to_chunks(lf)))
    h = h.swapaxes(0, 1).reshape(bsz, nc * CHUNK, M_HEADS, M_DV)[:, pad:]
    return h, c, n, m


def diff_lambda(lq1, lk1, lq2, lk2, lam_init):
    f = jnp.float32
    return (jnp.exp(jnp.sum(lq1.astype(f) * lk1.astype(f))) - jnp.exp(jnp.sum(lq2.astype(f) * lk2.astype(f))) + lam_init)


def diff_attn_prompt(q, k, v, lam):
    bsz, t = q.shape[:2]
    nb = -(-t // Q_BLOCK)
    pad = nb * Q_BLOCK - t
    qp = jnp.pad(q, ((0, 0), (0, pad), (0, 0), (0, 0), (0, 0)))
    qb = qp.reshape(bsz, nb, Q_BLOCK, A_HEADS, 2, A_DH).swapaxes(0, 1)
    qpos = jnp.arange(nb * Q_BLOCK).reshape(nb, Q_BLOCK)
    kf = k.astype(jnp.float32)
    vf = v.astype(jnp.float32)
    kpos = jnp.arange(t)
    scale = A_DH ** -0.5

    def blk(args):
        qi, qpi = args
        s = jnp.einsum('bqhcd,bkhcd->bhcqk', qi.astype(jnp.float32), kf) * scale
        s = jnp.where(kpos[None, :] <= qpi[:, None], s, -jnp.inf)
        p = jax.nn.softmax(s, axis=-1)
        a = p[:, :, 0] - lam * p[:, :, 1]
        return jnp.einsum('bhqk,bkhe->bqhe', a, vf)

    o = lax.map(blk, (qb, qpos))
    return o.swapaxes(0, 1).reshape(bsz, nb * Q_BLOCK, A_HEADS, A_DV)[:, :t]


def diff_attn_sample(q, k_new, v_new, cache_k, cache_v, page_table, layer, lam):
    s_dec = q.shape[1]
    past = page_table.shape[1] * PAGE_SIZE
    scale = A_DH ** -0.5
    kpos = jnp.arange(past + s_dec)
    qpos = past + jnp.arange(s_dec)

    def one(args):
        qi, ki, vi, pt = args
        kp = cache_k[pt, :, layer].reshape(past, A_HEADS, 2, A_DH).astype(jnp.float32)
        vp = cache_v[pt, :, layer].reshape(past, A_HEADS, A_DV).astype(jnp.float32)
        kk = jnp.concatenate([kp, ki.astype(jnp.float32)], axis=0)
        vv = jnp.concatenate([vp, vi.astype(jnp.float32)], axis=0)
        s = jnp.einsum('qhcd,khcd->hcqk', qi.astype(jnp.float32), kk) * scale
        s = jnp.where(kpos[None, :] <= qpos[:, None], s, -jnp.inf)
        p = jax.nn.softmax(s, axis=-1)
        a = p[:, 0] - lam * p[:, 1]
        return jnp.einsum('hqk,khe->qhe', a, vv)

    return lax.map(one, (q, k_new, v_new, page_table))


def merge_branches(h_m, m_o, o_a, g_a, g_b, mlstm_g, attn_g, lam_init, w_pa, w_pb, w_out):
    bsz, t = h_m.shape[:2]
    dt = g_a.dtype
    y_m = (head_rms(h_m).reshape(bsz, t, M_WIDTH) * mlstm_g.astype(jnp.float32) * jax.nn.sigmoid(m_o.astype(jnp.float32))).astype(dt)
    y_a = (head_rms(o_a) * attn_g.astype(jnp.float32) * (1.0 - lam_init)).reshape(bsz, t, A_WIDTH).astype(dt)
    mixed = jax.nn.sigmoid(g_a) * (y_m @ w_pa) + jax.nn.sigmoid(g_b) * (y_a @ w_pb)
    return mixed @ w_out


def sq_relu_mlp(xn, w_up, w_down):
    return jnp.square(jax.nn.relu(xn @ w_up)) @ w_down


def setup_inputs(seed: int = 0) -> dict:
    key = jax.random.key(seed)
    ks = jax.random.split(key, 32)
    f32 = jnp.float32
    n_pages = PAST_LEN // PAGE_SIZE
    n_phys = (DEC_BATCH * n_pages * 5) // 4

    def nrm(k, shape, scale):
        return scale * jax.random.normal(k, shape, f32)

    page_table = jax.random.permutation(ks[8], n_phys)[:DEC_BATCH * n_pages].reshape(DEC_BATCH, n_pages).astype(jnp.int32)
    b_i = nrm(ks[13], (DEPTH, M_HEADS), 0.1)
    b_f = jnp.linspace(FGATE_BIAS_LO, FGATE_BIAS_HI, M_HEADS, dtype=f32)[None, :] + nrm(ks[14], (DEPTH, M_HEADS), 0.1)
    return {
        'x_prompt': nrm(ks[0], (BATCH, SEQ, D_MODEL), 1.0),
        'x_sample': nrm(ks[1], (DEC_BATCH, DEC_SEQ, D_MODEL), 1.0),
        'cache_k': nrm(ks[2], (n_phys, PAGE_SIZE, DEPTH, A_HEADS, 2 * A_DH), 1.0),
        'cache_v': nrm(ks[3], (n_phys, PAGE_SIZE, DEPTH, A_HEADS, A_DV), 1.0),
        'state_C': nrm(ks[4], (DEPTH, DEC_BATCH, M_HEADS, M_DV, M_DK), 0.1),
        'state_n': nrm(ks[5], (DEPTH, DEC_BATCH, M_HEADS, M_DK), 0.5),
        'state_m': nrm(ks[6], (DEPTH, DEC_BATCH, M_HEADS), 1.0),
        'state_conv': nrm(ks[7], (DEPTH, DEC_BATCH, CONV_W - 1, QK_M), 1.0),
        'page_table': page_table,
        'meta_tokens': nrm(ks[9], (N_META, D_MODEL), 1.0),
        'norm_mix_g': 1.0 + nrm(ks[10], (DEPTH, D_MODEL), 0.02),
        'norm_ffn_g': 1.0 + nrm(ks[11], (DEPTH, D_MODEL), 0.02),
        'w_in': nrm(ks[12], (DEPTH, D_MODEL, D_IN), D_MODEL ** -0.5),
        'b_if': jnp.concatenate([b_i, b_f], axis=-1),
        'conv_w': nrm(ks[15], (DEPTH, CONV_W, QK_M), 0.5),
        'conv_b': nrm(ks[16], (DEPTH, QK_M), 0.02),
        'mlstm_norm_g': 1.0 + nrm(ks[17], (DEPTH, M_WIDTH), 0.02),
        'lambda_q1': nrm(ks[18], (DEPTH, A_DH), 0.1),
        'lambda_k1': nrm(ks[19], (DEPTH, A_DH), 0.1),
        'lambda_q2': nrm(ks[20], (DEPTH, A_DH), 0.1),
        'lambda_k2': nrm(ks[21], (DEPTH, A_DH), 0.1),
        'attn_norm_g': 1.0 + nrm(ks[22], (DEPTH, A_DV), 0.02),
        'w_proj_a': nrm(ks[23], (DEPTH, M_WIDTH, D_MODEL), M_WIDTH ** -0.5),
        'w_proj_b': nrm(ks[24], (DEPTH, A_WIDTH, D_MODEL), A_WIDTH ** -0.5),
        'w_out': nrm(ks[25], (DEPTH, D_MODEL, D_MODEL), D_MODEL ** -0.5),
        'w_up': nrm(ks[26], (DEPTH, D_MODEL, D_FF), D_MODEL ** -0.5),
        'w_down': nrm(ks[27], (DEPTH, D_FF, D_MODEL), D_FF ** -0.5),
        'final_norm_g': 1.0 + nrm(ks[28], (D_MODEL,), 0.02),
    }


def reference(x_prompt, x_sample, cache_k, cache_v, state_C, state_n, state_m, state_conv, page_table,
              meta_tokens, norm_mix_g, norm_ffn_g, w_in, b_if, conv_w, conv_b, mlstm_norm_g,
              lambda_q1, lambda_k1, lambda_q2, lambda_k2, attn_norm_g, w_proj_a, w_proj_b, w_out,
              w_up, w_down, final_norm_g):
    bsz = x_prompt.shape[0]
    dbsz, s_dec = x_sample.shape[:2]
    past = page_table.shape[1] * PAGE_SIZE
    meta = jnp.broadcast_to(meta_tokens.astype(x_prompt.dtype)[None], (bsz, N_META, D_MODEL))
    hp = jnp.concatenate([meta, x_prompt], axis=1)
    t = hp.shape[1]
    pos_p = jnp.arange(t)
    pos_s = past + jnp.arange(s_dec)
    hs = x_sample
    kp_l, vp_l, cp_l, np_l, mp_l, bp_l = [], [], [], [], [], []
    ks_l, vs_l, cs_l, ns_l, ms_l, bs_l = [], [], [], [], [], []
    for l in range(DEPTH):
        lam_init = 0.8 - 0.6 * math.exp(-0.3 * l)
        lam = diff_lambda(lambda_q1[l], lambda_k1[l], lambda_q2[l], lambda_k2[l], lam_init)
        xn = rmsnorm(hp, norm_mix_g[l])
        zero_buf = jnp.zeros((bsz, CONV_W - 1, QK_M), xn.dtype)
        (mq, mk, mv, mli, mlf, mo, aq, ak, av, ga, gb, buf) = branch_inputs(xn, pos_p, zero_buf, w_in[l], b_if[l], conv_w[l], conv_b[l])
        h_m, c_p, n_p, m_p = mlstm_prompt(mq, mk, mv, mli, mlf)
        o_a = diff_attn_prompt(aq, ak, av, lam)
        hp = hp + merge_branches(h_m, mo, o_a, ga, gb, mlstm_norm_g[l], attn_norm_g[l], lam_init, w_proj_a[l], w_proj_b[l], w_out[l])
        hp = hp + sq_relu_mlp(rmsnorm(hp, norm_ffn_g[l]), w_up[l], w_down[l])
        kp_l.append(ak.reshape(bsz, t, A_HEADS, 2 * A_DH))
        vp_l.append(av)
        cp_l.append(c_p)
        np_l.append(n_p)
        mp_l.append(m_p)
        bp_l.append(buf)
        xn = rmsnorm(hs, norm_mix_g[l])
        (mq, mk, mv, mli, mlf, mo, aq, ak, av, ga, gb, buf) = branch_inputs(xn, pos_s, state_conv[l], w_in[l], b_if[l], conv_w[l], conv_b[l])
        init = (state_C[l].astype(jnp.float32), state_n[l].astype(jnp.float32), state_m[l].astype(jnp.float32))
        (c_s, n_s, m_s), h_m = mlstm_chunk(init, (mq, mk, mv, mli, mlf))
        o_a = diff_attn_sample(aq, ak, av, cache_k, cache_v, page_table, l, lam)
        hs = hs + merge_branches(h_m, mo, o_a, ga, gb, mlstm_norm_g[l], attn_norm_g[l], lam_init, w_proj_a[l], w_proj_b[l], w_out[l])
        hs = hs + sq_relu_mlp(rmsnorm(hs, norm_ffn_g[l]), w_up[l], w_down[l])
        ks_l.append(ak.reshape(dbsz, s_dec, A_HEADS, 2 * A_DH))
        vs_l.append(av)
        cs_l.append(c_s)
        ns_l.append(n_s)
        ms_l.append(m_s)
        bs_l.append(buf)
    y_prompt = rmsnorm(hp[:, N_META:], final_norm_g)
    y_sample = rmsnorm(hs, final_norm_g)
    return (y_prompt, y_sample,
            jnp.stack(kp_l, axis=2), jnp.stack(vp_l, axis=2), jnp.stack(cp_l), jnp.stack(np_l), jnp.stack(mp_l), jnp.stack(bp_l),
            jnp.stack(ks_l, axis=2), jnp.stack(vs_l, axis=2), jnp.stack(cs_l), jnp.stack(ns_l), jnp.stack(ms_l), jnp.stack(bs_l))
```

```python
import functools
import math

import jax
import jax.numpy as jnp
from jax import lax
from jax.experimental import pallas as pl
from jax.experimental.pallas import tpu as pltpu

F32 = jnp.float32
BF16 = jnp.bfloat16

D_MODEL = 1024
N_META = 16
M_HEADS = 4
M_DK = 128
M_DV = 128
M_WIDTH = M_HEADS * M_DV
QK_M = 2 * M_HEADS * M_DK
CONV_W = 4
NEG_BIG = -1e30
A_HEADS = 8
A_DH = 64
A_DV = 2 * A_DH
A_QK = A_HEADS * 2 * A_DH
A_WIDTH = A_HEADS * A_DV
ROT_DIM = A_DH // 4
ROPE_THETA = 500000.0
D_FF = 4 * D_MODEL
EPS = 1e-6
LAM_INIT = 0.8 - 0.6 * math.exp(-0.3 * 0)

LANES = 128
SUBLANES = 8
VMEM_LIMIT_BYTES = 56 * 1024 * 1024

Z_QK = 0
Z_MV_BLK = 2
Z_MO_BLK = 3
Z_AQ_BLK = 2
Z_AK_BLK = 3
Z_AV_BLK = 4
Z_GA_BLK = 5
Z_GB_BLK = 6
Z_GATE_COL = 7 * 1024
Z_GATE_BLK = Z_GATE_COL // LANES
Z_COLS = Z_GATE_COL + LANES

CHUNK = 128


def _cparams(sem):
    return pltpu.CompilerParams(dimension_semantics=sem, vmem_limit_bytes=VMEM_LIMIT_BYTES)


def _pick(n, target):
    if n <= target:
        return n
    t = target
    while t >= SUBLANES:
        if n % t == 0 and t % SUBLANES == 0:
            return t
        t -= SUBLANES
    return n


def _proj_kernel(x_ref, g_ref, w_ref, o_ref, xn_ref):
    @pl.when(pl.program_id(1) == 0)
    def _():
        x = x_ref[...]
        ms = jnp.mean(x * x, axis=-1, keepdims=True)
        xn_ref[...] = (x * lax.rsqrt(ms + EPS) * g_ref[...]).astype(BF16)

    o_ref[...] = jnp.dot(xn_ref[...], w_ref[...], preferred_element_type=F32)


def _proj(x, g, w):
    m, d = x.shape
    n = w.shape[1]
    tm = _pick(m, 1024)
    tn = n // 3 if (n % (3 * LANES) == 0 and n // 3 <= 4096) else _pick(n, 2048)
    return pl.pallas_call(
        _proj_kernel,
        out_shape=jax.ShapeDtypeStruct((m, n), F32),
        grid=(m // tm, n // tn),
        in_specs=[pl.BlockSpec((tm, d), lambda i, j: (i, 0)),
                  pl.BlockSpec((1, d), lambda i, j: (0, 0)),
                  pl.BlockSpec((d, tn), lambda i, j: (0, j))],
        out_specs=pl.BlockSpec((tm, tn), lambda i, j: (i, j)),
        scratch_shapes=[pltpu.VMEM((tm, d), BF16)],
        compiler_params=_cparams(("parallel", "arbitrary")),
        name="proj",
    )(x, g, w)


def _rope_apply(x, c, sa, sb):
    reps = x.shape[1] // LANES
    c = jnp.concatenate([c] * reps, axis=1)
    sa = jnp.concatenate([sa] * reps, axis=1)
    sb = jnp.concatenate([sb] * reps, axis=1)
    half = ROT_DIM // 2
    up = pltpu.roll(x, half, axis=1)
    dn = pltpu.roll(x, x.shape[1] - half, axis=1)
    return x * c + dn * sb + up * sa


def _rope_kernel(q_ref, k_ref, v_ref, c_ref, sa_ref, sb_ref, qo_ref, ko_ref, kb_ref, vb_ref):
    c, sa, sb = c_ref[...], sa_ref[...], sb_ref[...]
    q = _rope_apply(q_ref[...], c, sa, sb)
    qo_ref[...] = (q * (A_DH ** -0.5)).astype(BF16)
    k = _rope_apply(k_ref[...], c, sa, sb)
    ko_ref[...] = k
    kb_ref[...] = k.astype(BF16)
    vb_ref[...] = v_ref[...].astype(BF16)


def _rope_tables(pos):
    half = ROT_DIM // 2
    inv = ROPE_THETA ** (-jnp.arange(0, ROT_DIM, 2, dtype=F32) / ROT_DIM)
    ang = pos.astype(F32)[:, None] * inv[None, :]
    cos, sin = jnp.cos(ang), jnp.sin(ang)
    p = pos.shape[0]
    one = jnp.ones((p, A_DH - ROT_DIM), F32)
    zero = jnp.zeros((p, A_DH - ROT_DIM), F32)
    zh = jnp.zeros((p, half), F32)
    c = jnp.concatenate([cos, cos, one], axis=1)
    sa = jnp.concatenate([zh, sin, zero], axis=1)
    sb = jnp.concatenate([-sin, zh, zero], axis=1)
    tile2 = lambda a: jnp.concatenate([a, a], axis=1)
    return tile2(c), tile2(sa), tile2(sb)


def _rope(z, tables, rows_per_table):
    m = z.shape[0]
    tm = _pick(rows_per_table, 512)
    nt = rows_per_table // tm
    zspec = lambda blk: pl.BlockSpec((tm, A_QK), lambda i, blk=blk: (i, blk))
    tspec = pl.BlockSpec((tm, LANES), lambda i: (i % nt, 0))
    ospec = pl.BlockSpec((tm, A_QK), lambda i: (i, 0))
    return pl.pallas_call(
        _rope_kernel,
        out_shape=(jax.ShapeDtypeStruct((m, A_QK), BF16), jax.ShapeDtypeStruct((m, A_QK), F32),
                   jax.ShapeDtypeStruct((m, A_QK), BF16), jax.ShapeDtypeStruct((m, A_WIDTH), BF16)),
        grid=(m // tm,),
        in_specs=[zspec(Z_AQ_BLK), zspec(Z_AK_BLK), zspec(Z_AV_BLK), tspec, tspec, tspec],
        out_specs=(ospec, ospec, ospec, ospec),
        compiler_params=_cparams(("parallel",)),
        name="rope",
    )(z, z, z, *tables)


def _log_sigmoid(x):
    return -(jnp.maximum(-x, 0.0) + jnp.log1p(jnp.exp(-jnp.abs(x))))


def _sigmoid(x):
    return 1.0 / (1.0 + jnp.exp(-x))


def _dot_hilo(a_bf, x):
    hi = x.astype(BF16)
    lo = (x - hi.astype(F32)).astype(BF16)
    return (jnp.dot(a_bf, hi, preferred_element_type=F32) + jnp.dot(a_bf, lo, preferred_element_type=F32))


def _dot_hilo_r(x, a_bf):
    hi = x.astype(BF16)
    lo = (x - hi.astype(F32)).astype(BF16)
    return (jnp.dot(hi, a_bf, preferred_element_type=F32) + jnp.dot(lo, a_bf, preferred_element_type=F32))


def _mlstm_kernel(n_valid, u_ref, mv_ref, mo_ref, gc_ref, gr_ref, cinit_ref, c0_ref, n0_ref, m0_ref,
                  cw_ref, cb_ref, bc_ref, br_ref, ng_ref,
                  y_ref, co_ref, no_ref, mo_out_ref,
                  ubuf, c_s, n_s, m_s):
    L = u_ref.shape[0]
    c_id = pl.program_id(1)

    @pl.when(c_id == 0)
    def _():
        ubuf[0:SUBLANES, :] = cinit_ref[...]
        c_s[...] = c0_ref[...]
        n_s[...] = n0_ref[...]
        m_s[...] = m0_ref[...]

    ubuf[SUBLANES:SUBLANES + L, :] = u_ref[...]
    y = ubuf[SUBLANES - 3:SUBLANES - 3 + L, :] * cw_ref[0:1, :] + cb_ref[...]
    for j in range(1, CONV_W):
        y = y + ubuf[SUBLANES - 3 + j:SUBLANES - 3 + j + L, :] * cw_ref[j:j + 1, :]
    qk = y * _sigmoid(y)
    ubuf[0:SUBLANES, :] = ubuf[L:L + SUBLANES, :]

    gcol = gc_ref[...] + bc_ref[...]
    grow = gr_ref[...] + br_ref[...]
    li_c, lf_c = gcol, _log_sigmoid(gcol)
    li_r, lf_r = grow, _log_sigmoid(grow)
    if n_valid < L:
        vc = lax.broadcasted_iota(jnp.int32, gcol.shape, 0) < n_valid
        vr = lax.broadcasted_iota(jnp.int32, grow.shape, 1) < n_valid
        li_c, lf_c = jnp.where(vc, li_c, NEG_BIG), jnp.where(vc, lf_c, 0.0)
        li_r, lf_r = jnp.where(vr, li_r, NEG_BIG), jnp.where(vr, lf_r, 0.0)
    t_i = lax.broadcasted_iota(jnp.int32, (L, L), 0)
    s_i = lax.broadcasted_iota(jnp.int32, (L, L), 1)
    tril = s_i <= t_i
    tri_lo = jnp.where(tril, 1.0, 0.0).astype(BF16)
    tri_up = jnp.where(t_i <= s_i, 1.0, 0.0).astype(BF16)
    b_c = _dot_hilo(tri_lo, lf_c)
    b_r = _dot_hilo_r(lf_r, tri_up)

    for h in range(M_HEADS):
        hs = slice(h * M_DK, (h + 1) * M_DK)
        q = qk[:, hs]
        k = qk[:, M_HEADS * M_DK + h * M_DK:M_HEADS * M_DK + (h + 1) * M_DK] * (M_DK ** -0.5)
        v = mv_ref[:, hs]
        qb, kb = q.astype(BF16), k.astype(BF16)
        a_row = li_r[h:h + 1, :] - b_r[M_HEADS + h:M_HEADS + h + 1, :]
        b_col = b_c[:, M_HEADS + h:M_HEADS + h + 1]
        a_col = li_c[:, h:h + 1] - b_col
        m_prev = m_s[h]
        c_prev = c_s[h]
        n_prev = n_s[h]
        a_mat = jnp.where(tril, jnp.broadcast_to(a_row, (L, L)), -jnp.inf)
        g_col = jnp.maximum(m_prev, jnp.max(a_mat, axis=1, keepdims=True))
        d_mat = jnp.exp(a_mat - g_col)
        w_inter = jnp.exp(m_prev - g_col)
        s_mat = lax.dot_general(qb, kb, (((1,), (1,)), ((), ())), preferred_element_type=F32) * d_mat
        cq = lax.dot_general(qb, c_prev.astype(BF16), (((1,), (1,)), ((), ())), preferred_element_type=F32)
        num = jnp.dot(s_mat.astype(BF16), v.astype(BF16), preferred_element_type=F32) + w_inter * cq
        den = jnp.sum(s_mat, axis=1, keepdims=True) + w_inter * jnp.sum(q * n_prev, axis=1, keepdims=True)
        m_t = b_col + g_col
        hh = num / jnp.maximum(jnp.abs(den), jnp.exp(-m_t))
        g_last = g_col[L - 1:L, :]
        w_col = jnp.exp(a_col - g_last)
        decay = jnp.exp(m_prev - g_last)
        vw = (v * w_col).astype(BF16)
        c_s[h] = decay * c_prev + lax.dot_general(vw, kb, (((0,), (0,)), ((), ())), preferred_element_type=F32)
        n_s[h] = decay * n_prev + jnp.sum(k * w_col, axis=0, keepdims=True)
        m_s[h] = b_col[L - 1:L, :] + g_last
        hn = hh * lax.rsqrt(jnp.mean(hh * hh, axis=-1, keepdims=True) + EPS)
        y_ref[:, hs] = (hn * ng_ref[:, hs] * _sigmoid(mo_ref[:, hs])).astype(y_ref.dtype)

    @pl.when(c_id == pl.num_programs(1) - 1)
    def _():
        co_ref[...] = c_s[...]
        no_ref[...] = n_s[...]
        mo_out_ref[...] = m_s[...]


def _mlstm(z, gates_t, conv_init, c0, n0, m0, prm, nb, nc, n_valid):
    L = CHUNK
    rows = nb * nc * L
    bsel = (lambda b: b) if c0.shape[0] == nb else (lambda b: 0)
    csel = (lambda b: b) if conv_init.shape[0] == nb else (lambda b: 0)
    row = lambda b, c: b * nc + c
    const = lambda shape: pl.BlockSpec(shape, lambda b, c: (0,) * len(shape))
    in_specs = [
        pl.BlockSpec((L, QK_M), lambda b, c: (row(b, c), 0)),
        pl.BlockSpec((L, M_WIDTH), lambda b, c: (row(b, c), Z_MV_BLK)),
        pl.BlockSpec((L, M_WIDTH), lambda b, c: (row(b, c), Z_MO_BLK)),
        pl.BlockSpec((L, LANES), lambda b, c: (row(b, c), Z_GATE_BLK)),
        pl.BlockSpec((SUBLANES, L), lambda b, c: (0, row(b, c))),
        pl.BlockSpec((None, SUBLANES, QK_M), lambda b, c: (csel(b), 0, 0)),
        pl.BlockSpec((None, M_HEADS, M_DV, M_DK), lambda b, c: (bsel(b), 0, 0, 0)),
        pl.BlockSpec((None, M_HEADS, 1, M_DK), lambda b, c: (bsel(b), 0, 0, 0)),
        pl.BlockSpec((None, M_HEADS, 1, 1), lambda b, c: (bsel(b), 0, 0, 0)),
        const((CONV_W, QK_M)), const((1, QK_M)), const((1, LANES)), const((SUBLANES, L)), const((1, M_WIDTH)),
    ]
    out_specs = (
        pl.BlockSpec((L, M_WIDTH), lambda b, c: (row(b, c), 0)),
        pl.BlockSpec((None, M_HEADS, M_DV, M_DK), lambda b, c: (b, 0, 0, 0)),
        pl.BlockSpec((None, M_HEADS, 1, M_DK), lambda b, c: (b, 0, 0, 0)),
        pl.BlockSpec((None, M_HEADS, 1, 1), lambda b, c: (b, 0, 0, 0)),
    )
    out_shape = (
        jax.ShapeDtypeStruct((rows, M_WIDTH), BF16),
        jax.ShapeDtypeStruct((nb, M_HEADS, M_DV, M_DK), F32),
        jax.ShapeDtypeStruct((nb, M_HEADS, 1, M_DK), F32),
        jax.ShapeDtypeStruct((nb, M_HEADS, 1, 1), F32),
    )
    return pl.pallas_call(
        functools.partial(_mlstm_kernel, n_valid),
        out_shape=out_shape,
        grid=(nb, nc),
        in_specs=in_specs,
        out_specs=out_specs,
        scratch_shapes=[pltpu.VMEM((L + SUBLANES, QK_M), F32), pltpu.VMEM((M_HEADS, M_DV, M_DK), F32),
                        pltpu.VMEM((M_HEADS, 1, M_DK), F32), pltpu.VMEM((M_HEADS, 1, 1), F32)],
        compiler_params=_cparams(("parallel", "arbitrary")),
        name="mlstm",
    )(z, z, z, z, gates_t, conv_init, c0, n0, m0,
      prm["conv_w"], prm["conv_b"], prm["b_if_row"], prm["b_if_col"], prm["mlstm_g"])


def _mstep_kernel(u_ref, mv_ref, mo_ref, g_ref, sc_ref, c0_ref, n0_ref, m0_ref,
                  cw_ref, cb_ref, bif_ref, ng_ref,
                  y_ref, co_ref, no_ref, mo_out_ref, conv_ref):
    u = u_ref[...]
    sc = sc_ref[...]
    y = sc[0:1, :] * cw_ref[0:1, :] + cb_ref[...]
    y = y + sc[1:2, :] * cw_ref[1:2, :]
    y = y + sc[2:3, :] * cw_ref[2:3, :]
    y = y + u * cw_ref[3:4, :]
    qk = y * _sigmoid(y)
    conv_ref[0:2, :] = sc[1:3, :]
    conv_ref[2:3, :] = u
    gates = g_ref[...] + bif_ref[...]
    lf_all = _log_sigmoid(gates)
    e_i = lax.broadcasted_iota(jnp.int32, (M_DV, M_DV), 0)
    e_j = lax.broadcasted_iota(jnp.int32, (M_DV, M_DV), 1)
    eye = e_i == e_j
    for h in range(M_HEADS):
        hs = slice(h * M_DK, (h + 1) * M_DK)
        q = qk[:, hs]
        k = qk[:, M_HEADS * M_DK + h * M_DK:M_HEADS * M_DK + (h + 1) * M_DK] * (M_DK ** -0.5)
        v = mv_ref[:, hs]
        li = gates[:, h:h + 1]
        lf = lf_all[:, M_HEADS + h:M_HEADS + h + 1]
        m_prev = m0_ref[h]
        c_prev = c0_ref[h]
        n_prev = n0_ref[h]
        m_t = jnp.maximum(lf + m_prev, li)
        d_w = jnp.exp(li - m_t)
        w_inter = jnp.exp(lf + m_prev - m_t)
        s = jnp.sum(q * k, axis=1, keepdims=True) * d_w
        q8 = jnp.broadcast_to(q, (SUBLANES, M_DK)).astype(BF16)
        cq = lax.dot_general(q8, c_prev.astype(BF16), (((1,), (1,)), ((), ())), preferred_element_type=F32)[0:1, :]
        num = s * v + w_inter * cq
        den = s + w_inter * jnp.sum(q * n_prev, axis=1, keepdims=True)
        hh = num / jnp.maximum(jnp.abs(den), jnp.exp(-m_t))
        vdiag = jnp.where(eye, jnp.broadcast_to(v * d_w, (M_DV, M_DV)), 0.0).astype(BF16)
        krows = jnp.broadcast_to(k, (M_DV, M_DK)).astype(BF16)
        co_ref[h] = w_inter * c_prev + jnp.dot(vdiag, krows, preferred_element_type=F32)
        no_ref[h] = w_inter * n_prev + d_w * k
        mo_out_ref[h] = m_t
        hn = hh * lax.rsqrt(jnp.mean(hh * hh, axis=-1, keepdims=True) + EPS)
        y_ref[:, hs] = (hn * ng_ref[:, hs] * _sigmoid(mo_ref[:, hs])).astype(y_ref.dtype)


def _mstep(z3, state_conv, c0, n0, m0, prm):
    nb = z3.shape[0]
    const = lambda shape: pl.BlockSpec(shape, lambda b: (0,) * len(shape))
    zspec = lambda w, blk: pl.BlockSpec((None, 1, w), lambda b, blk=blk: (b, 0, blk))
    st = lambda shape: pl.BlockSpec((None,) + shape, lambda b: (b,) + (0,) * len(shape))
    return pl.pallas_call(
        _mstep_kernel,
        out_shape=(jax.ShapeDtypeStruct((nb, 1, M_WIDTH), BF16),
                   jax.ShapeDtypeStruct((nb, M_HEADS, M_DV, M_DK), F32),
                   jax.ShapeDtypeStruct((nb, M_HEADS, 1, M_DK), F32),
                   jax.ShapeDtypeStruct((nb, M_HEADS, 1, 1), F32),
                   jax.ShapeDtypeStruct((nb, CONV_W - 1, QK_M), F32)),
        grid=(nb,),
        in_specs=[zspec(QK_M, 0), zspec(M_WIDTH, Z_MV_BLK), zspec(M_WIDTH, Z_MO_BLK), zspec(LANES, Z_GATE_BLK),
                  st((CONV_W - 1, QK_M)), st((M_HEADS, M_DV, M_DK)), st((M_HEADS, 1, M_DK)), st((M_HEADS, 1, 1)),
                  const((CONV_W, QK_M)), const((1, QK_M)), const((1, LANES)), const((1, M_WIDTH))],
        out_specs=(pl.BlockSpec((None, 1, M_WIDTH), lambda b: (b, 0, 0)),
                   st((M_HEADS, M_DV, M_DK)), st((M_HEADS, 1, M_DK)), st((M_HEADS, 1, 1)), st((CONV_W - 1, QK_M))),
        compiler_params=_cparams(("parallel",)),
        name="mstep",
    )(z3, z3, z3, z3, state_conv, c0, n0, m0,
      prm["conv_w"], prm["conv_b"], prm["b_if_row"], prm["mlstm_g"])


def _diff_lambda(l_ref):
    a = jnp.sum(l_ref[0:1, :] * l_ref[1:2, :], axis=1, keepdims=True)
    b = jnp.sum(l_ref[2:3, :] * l_ref[3:4, :], axis=1, keepdims=True)
    return jnp.exp(a) - jnp.exp(b) + LAM_INIT


def _attn_kernel(n_meta, q_ref, k_ref, v_ref, km_ref, vm_ref, lam_ref, ag_ref, o_ref, m_s, l_s, acc_s):
    tq = q_ref.shape[0]
    i = pl.program_id(2)
    q = q_ref[...]
    lane = lax.broadcasted_iota(jnp.int32, q.shape, 1)
    zero = jnp.zeros_like(q)
    qq = jnp.concatenate([jnp.where(lane < A_DH, q, zero), jnp.where(lane >= A_DH, q, zero)], axis=0)
    nt = (((1,), (1,)), ((), ()))

    s = lax.dot_general(qq, km_ref[...], nt, preferred_element_type=F32)
    col = lax.broadcasted_iota(jnp.int32, s.shape, 1)
    s = jnp.where(col < n_meta, s, -jnp.inf)
    m0 = jnp.max(s, axis=1, keepdims=True)
    p = jnp.exp(s - m0)
    m_s[...] = m0
    l_s[...] = jnp.sum(p, axis=1, keepdims=True)
    acc_s[...] = jnp.dot(p.astype(BF16), vm_ref[...], preferred_element_type=F32)

    def step(j, masked):
        start = pl.multiple_of(j * tq, tq)
        kc = k_ref[pl.ds(start, tq), :]
        vc = v_ref[pl.ds(start, tq), :]
        s = lax.dot_general(qq, kc, nt, preferred_element_type=F32)
        if masked:
            r = lax.broadcasted_iota(jnp.int32, s.shape, 0)
            r = jnp.where(r >= tq, r - tq, r)
            c = lax.broadcasted_iota(jnp.int32, s.shape, 1)
            s = jnp.where(c <= r, s, -jnp.inf)
        m_old = m_s[...]
        m_new = jnp.maximum(m_old, jnp.max(s, axis=1, keepdims=True))
        alpha = jnp.exp(m_old - m_new)
        p = jnp.exp(s - m_new)
        l_s[...] = alpha * l_s[...] + jnp.sum(p, axis=1, keepdims=True)
        acc_s[...] = alpha * acc_s[...] + jnp.dot(p.astype(BF16), vc, preferred_element_type=F32)
        m_s[...] = m_new

    def body(j, carry):
        step(j, False)
        return carry

    lax.fori_loop(0, i, body, 0)
    step(i, True)

    lam = _diff_lambda(lam_ref)
    o = acc_s[...] / l_s[...]
    o = o[0:tq, :] - lam * o[tq:2 * tq, :]
    on = o * lax.rsqrt(jnp.mean(o * o, axis=-1, keepdims=True) + EPS)
    o_ref[...] = (on * ag_ref[...] * (1.0 - LAM_INIT)).astype(o_ref.dtype)


def _attn(q, k, v, k_meta, v_meta, lam4, attn_g, n_meta):
    bsz, t, _ = q.shape
    tq = _pick(t, 512)
    hspec = lambda rows: pl.BlockSpec((None, rows, LANES), lambda b, h, i: (b, 0, h))
    return pl.pallas_call(
        functools.partial(_attn_kernel, n_meta),
        out_shape=jax.ShapeDtypeStruct((bsz, t, A_WIDTH), BF16),
        grid=(bsz, A_HEADS, t // tq),
        in_specs=[pl.BlockSpec((None, tq, LANES), lambda b, h, i: (b, i, h)),
                  hspec(t), hspec(t),
                  pl.BlockSpec((LANES, LANES), lambda b, h, i: (0, h)),
                  pl.BlockSpec((LANES, LANES), lambda b, h, i: (0, h)),
                  pl.BlockSpec((4, A_DH), lambda b, h, i: (0, 0)),
                  pl.BlockSpec((1, A_DV), lambda b, h, i: (0, 0))],
        out_specs=pl.BlockSpec((None, tq, LANES), lambda b, h, i: (b, i, h)),
        scratch_shapes=[pltpu.VMEM((2 * tq, 1), F32), pltpu.VMEM((2 * tq, 1), F32), pltpu.VMEM((2 * tq, A_DV), F32)],
        compiler_params=_cparams(("parallel", "parallel", "arbitrary")),
        name="attn",
    )(q, k, v, k_meta, v_meta, lam4, attn_g)


def _class_allreduce(x, op):
    for sh in (A_HEADS, 2 * A_HEADS, 4 * A_HEADS, 8 * A_HEADS):
        x = op(x, pltpu.roll(x, sh, axis=1))
    return x


def _decode_kernel(g_pages, *refs):
    pt_ref = refs[0]
    del pt_ref
    k_refs = refs[1:1 + g_pages]
    v_refs = refs[1 + g_pages:1 + 2 * g_pages]
    q_ref, kn_ref, vn_ref, lam_ref, ag_ref, o_ref, m_s, l_s, acc_s = refs[1 + 2 * g_pages:]
    p_id = pl.program_id(1)
    nt = (((1,), (1,)), ((), ()))
    q = q_ref[...]
    r_i = lax.broadcasted_iota(jnp.int32, (SUBLANES, LANES), 0)
    l_i = lax.broadcasted_iota(jnp.int32, (SUBLANES, LANES), 1)
    rsel = jnp.where((r_i < 2) & ((l_i >= A_DH) == (r_i == 1)), 1.0, 0.0).astype(BF16)
    diag = (l_i % A_HEADS) == r_i

    def colvec(x_rep):
        cols = []
        for c in range(2):
            xb = jnp.broadcast_to(x_rep[c:c + 1, :], (SUBLANES, LANES))
            cols.append(jnp.sum(jnp.where(diag & (l_i < A_HEADS), xb, 0.0), axis=1, keepdims=True))
        return jnp.concatenate(cols, axis=0)

    def block(kmat, vmat, n_lanes_valid):
        rows = kmat.shape[0]
        st = lax.dot_general(rsel, kmat.astype(BF16), nt, preferred_element_type=F32)
        nv = rows // LANES
        tiles = [st[:, t * LANES:(t + 1) * LANES] for t in range(nv)]
        if n_lanes_valid is not None:
            tiles = [jnp.where(l_i + t * LANES < n_lanes_valid, x, -jnp.inf) for t, x in enumerate(tiles)]
        mb = tiles[0]
        for x in tiles[1:]:
            mb = jnp.maximum(mb, x)
        mb = _class_allreduce(mb, jnp.maximum)
        m_old = m_s[...]
        m_new = jnp.maximum(m_old, mb)
        alpha = jnp.exp(m_old - m_new)
        ps = [jnp.exp(x - m_new) for x in tiles]
        lsum = ps[0]
        for x in ps[1:]:
            lsum = lsum + x
        l_s[...] = alpha * l_s[...] + lsum
        m_s[...] = m_new
        pm = []
        for c in range(2):
            pm.append(jnp.concatenate(
                [jnp.where(diag, jnp.broadcast_to(x[c:c + 1, :], (SUBLANES, LANES)), 0.0) for x in ps], axis=1))
        pm = jnp.concatenate(pm, axis=0).astype(BF16)
        pv = jnp.dot(pm, vmat.astype(BF16), preferred_element_type=F32)
        acc_s[...] = colvec(alpha) * acc_s[...] + pv

    @pl.when(p_id == 0)
    def _():
        m_s[...] = jnp.full(m_s.shape, -jnp.inf, F32)
        l_s[...] = jnp.zeros(l_s.shape, F32)
        acc_s[...] = jnp.zeros(acc_s.shape, F32)
        pad = jnp.zeros((LANES - SUBLANES, LANES), F32)
        m_s[...] = jnp.full(m_s.shape, NEG_BIG, F32)
        block(jnp.concatenate([kn_ref[...] * q, pad], axis=0), jnp.concatenate([vn_ref[...], pad], axis=0), A_HEADS)

    kmat = jnp.concatenate([(r[...] * q[None, :, :]).reshape(-1, LANES) for r in k_refs], axis=0)
    vmat = jnp.concatenate([r[...].reshape(-1, LANES) for r in v_refs], axis=0)
    block(kmat, vmat, None)

    @pl.when(p_id == pl.num_programs(1) - 1)
    def _():
        lam = _diff_lambda(lam_ref)
        l_rep = _class_allreduce(l_s[...], jnp.add)
        o = acc_s[...] / colvec(l_rep)
        o = o[0:A_HEADS, :] - lam * o[A_HEADS:2 * A_HEADS, :]
        on = o * lax.rsqrt(jnp.mean(o * o, axis=-1, keepdims=True) + EPS)
        o_ref[...] = (on * ag_ref[...] * (1.0 - LAM_INIT)).astype(o_ref.dtype)


def _decode_attn(q, k_new, v_new, cache_k, cache_v, page_table, lam4, attn_g):
    db, n_pages = page_table.shape
    page = cache_k.shape[1]
    g_pages = 8 if n_pages % 8 == 0 else (2 if n_pages % 2 == 0 else 1)

    def pspec(g):
        return pl.BlockSpec((None, page, None, A_HEADS, A_DV),
                            lambda b, p, pt, g=g: (pt[b, p * g_pages + g], 0, 0, 0, 0))

    row = pl.BlockSpec((None, A_HEADS, A_DV), lambda b, p, pt: (b, 0, 0))
    grid_spec = pltpu.PrefetchScalarGridSpec(
        num_scalar_prefetch=1,
        grid=(db, n_pages // g_pages),
        in_specs=[pspec(g) for g in range(g_pages)] + [pspec(g) for g in range(g_pages)] + [
            row, row, row,
            pl.BlockSpec((4, A_DH), lambda b, p, pt: (0, 0)),
            pl.BlockSpec((1, A_DV), lambda b, p, pt: (0, 0))],
        out_specs=row,
        scratch_shapes=[pltpu.VMEM((SUBLANES, LANES), F32), pltpu.VMEM((SUBLANES, LANES), F32),
                        pltpu.VMEM((2 * A_HEADS, A_DV), F32)],
    )
    return pl.pallas_call(
        functools.partial(_decode_kernel, g_pages),
        out_shape=jax.ShapeDtypeStruct((db, A_HEADS, A_DV), BF16),
        grid_spec=grid_spec,
        compiler_params=_cparams(("parallel", "arbitrary")),
        name="decode",
    )(page_table, *([cache_k] * g_pages), *([cache_v] * g_pages), q, k_new, v_new, lam4, attn_g)


def _merge_kernel(x_ref, ym_ref, ya_ref, ga_ref, gb_ref, wpa_ref, wpb_ref, wo_ref, o_ref):
    pa = jnp.dot(ym_ref[...], wpa_ref[...], preferred_element_type=F32)
    pb = jnp.dot(ya_ref[...], wpb_ref[...], preferred_element_type=F32)
    mixed = _sigmoid(ga_ref[...]) * pa + _sigmoid(gb_ref[...]) * pb
    o_ref[...] = x_ref[...] + jnp.dot(mixed.astype(BF16), wo_ref[...], preferred_element_type=F32)


def _merge(x, y_m, y_a, z, prm):
    m = x.shape[0]
    tm = _pick(m, 512)
    rspec = lambda w, blk=0: pl.BlockSpec((tm, w), lambda i, blk=blk: (i, blk))
    const = lambda shape: pl.BlockSpec(shape, lambda i: (0, 0))
    return pl.pallas_call(
        _merge_kernel,
        out_shape=jax.ShapeDtypeStruct((m, D_MODEL), F32),
        grid=(m // tm,),
        in_specs=[rspec(D_MODEL), rspec(M_WIDTH), rspec(A_WIDTH), rspec(D_MODEL, Z_GA_BLK), rspec(D_MODEL, Z_GB_BLK),
                  const((M_WIDTH, D_MODEL)), const((A_WIDTH, D_MODEL)), const((D_MODEL, D_MODEL))],
        out_specs=rspec(D_MODEL),
        compiler_params=_cparams(("parallel",)),
        name="merge",
    )(x, y_m, y_a, z, z, prm["w_pa"], prm["w_pb"], prm["w_out"])


def _mlp_kernel(x_ref, g1_ref, wu_ref, wd_ref, g2_ref, o_ref):
    x = x_ref[...]
    xn = (x * lax.rsqrt(jnp.mean(x * x, axis=-1, keepdims=True) + EPS) * g1_ref[...]).astype(BF16)
    acc = x
    fc = D_MODEL
    for f in range(D_FF // fc):
        hf = jnp.dot(xn, wu_ref[:, f * fc:(f + 1) * fc], preferred_element_type=F32)
        hf = jnp.square(jnp.maximum(hf, 0.0)).astype(BF16)
        acc = acc + jnp.dot(hf, wd_ref[f * fc:(f + 1) * fc, :], preferred_element_type=F32)
    o_ref[...] = acc * lax.rsqrt(jnp.mean(acc * acc, axis=-1, keepdims=True) + EPS) * g2_ref[...]


def _mlp(x, prm):
    m = x.shape[0]
    tm = _pick(m, 512)
    const = lambda shape: pl.BlockSpec(shape, lambda i: (0, 0))
    return pl.pallas_call(
        _mlp_kernel,
        out_shape=jax.ShapeDtypeStruct((m, D_MODEL), F32),
        grid=(m // tm,),
        in_specs=[pl.BlockSpec((tm, D_MODEL), lambda i: (i, 0)), const((1, D_MODEL)),
                  const((D_MODEL, D_FF)), const((D_FF, D_MODEL)), const((1, D_MODEL))],
        out_specs=pl.BlockSpec((tm, D_MODEL), lambda i: (i, 0)),
        compiler_params=_cparams(("parallel",)),
        name="mlp",
    )(x, prm["ffn_g"], prm["w_up"], prm["w_down"], prm["final_g"])


def kernel(x_prompt, x_sample, cache_k, cache_v, state_C, state_n, state_m, state_conv, page_table, meta_tokens, norm_mix_g, norm_ffn_g, w_in, b_if, conv_w, conv_b, mlstm_norm_g, lambda_q1, lambda_k1, lambda_q2, lambda_k2, attn_norm_g, w_proj_a, w_proj_b, w_out, w_up, w_down, final_norm_g):
    bsz, seq, d = x_prompt.shape
    dbsz, s_dec, _ = x_sample.shape
    n_pages = page_table.shape[1]
    past = n_pages * cache_k.shape[1]
    assert d == D_MODEL and s_dec == 1 and seq % CHUNK == 0 and norm_mix_g.shape[0] == 1
    assert N_META + dbsz <= CHUNK
    l = 0

    gate_lo = QK_M + 2 * M_WIDTH
    gate_hi = gate_lo + 2 * M_HEADS
    w = w_in[l]
    w_re = jnp.concatenate([w[:, :gate_lo], w[:, gate_hi:], w[:, gate_lo:gate_hi],
                            jnp.zeros((d, LANES - 2 * M_HEADS), w.dtype)], axis=1).astype(BF16)
    bif = b_if[l].astype(F32)
    prm = {
        "conv_w": conv_w[l].astype(F32), "conv_b": conv_b[l].astype(F32)[None, :],
        "b_if_row": jnp.pad(bif, (0, LANES - 2 * M_HEADS))[None, :],
        "b_if_col": jnp.broadcast_to(bif[:, None], (2 * M_HEADS, CHUNK)),
        "mlstm_g": mlstm_norm_g[l].astype(F32)[None, :],
        "w_pa": w_proj_a[l].astype(BF16), "w_pb": w_proj_b[l].astype(BF16), "w_out": w_out[l].astype(BF16),
        "ffn_g": norm_ffn_g[l].astype(F32)[None, :], "w_up": w_up[l].astype(BF16), "w_down": w_down[l].astype(BF16),
        "final_g": final_norm_g.astype(F32)[None, :],
    }
    mix_g = norm_mix_g[l].astype(F32)[None, :]
    lam4 = jnp.stack([lambda_q1[l], lambda_k1[l], lambda_q2[l], lambda_k2[l]]).astype(F32)
    attn_g = attn_norm_g[l].astype(F32)[None, :]

    xp = x_prompt.reshape(bsz * seq, d)
    xs = x_sample.reshape(dbsz, d)
    xe = jnp.concatenate([meta_tokens.astype(F32), xs, jnp.zeros((CHUNK - N_META - dbsz, d), F32)], axis=0)
    z_p = _proj(xp, mix_g, w_re)
    z_e = _proj(xe, mix_g, w_re)

    pos_e = jnp.concatenate([jnp.arange(N_META), jnp.full((dbsz,), past), jnp.zeros((CHUNK - N_META - dbsz,), jnp.int32)])
    q_p, k_p, kb_p, vb_p = _rope(z_p, _rope_tables(N_META + jnp.arange(seq)), seq)
    q_e, k_e, kb_e, vb_e = _rope(z_e, _rope_tables(pos_e), CHUNK)

    zero_state = (jnp.zeros((1, M_HEADS, M_DV, M_DK), F32), jnp.zeros((1, M_HEADS, 1, M_DK), F32),
                  jnp.zeros((1, M_HEADS, 1, 1), F32))
    gt_e = z_e[:, Z_GATE_COL:Z_GATE_COL + 2 * M_HEADS].T
    gt_p = z_p[:, Z_GATE_COL:Z_GATE_COL + 2 * M_HEADS].T
    _, c_m, n_m, m_m = _mlstm(z_e, gt_e, jnp.zeros((1, SUBLANES, QK_M), F32), *zero_state, prm, 1, 1, N_META)
    conv_init = z_e[N_META - SUBLANES:N_META, :QK_M][None]
    ym_p, c_p, n_p, m_p = _mlstm(z_p, gt_p, conv_init, c_m, n_m, m_m, prm, bsz, seq // CHUNK, CHUNK)
    z_s = z_e[N_META:N_META + dbsz]
    ym_s, c_s, n_s, m_s, conv_s = _mstep(
        z_s[:, None, :], state_conv[l].astype(F32), state_C[l].astype(F32),
        state_n[l].astype(F32)[:, :, None, :], state_m[l].astype(F32)[:, :, None, None], prm)

    ya_p = _attn(q_p.reshape(bsz, seq, A_QK), kb_p.reshape(bsz, seq, A_QK), vb_p.reshape(bsz, seq, A_WIDTH),
                 kb_e, vb_e, lam4, attn_g, N_META)
    hv = lambda a: a[N_META:N_META + dbsz].reshape(dbsz, A_HEADS, A_DV)
    v_s = z_s[:, Z_AV_BLK * A_QK:(Z_AV_BLK + 1) * A_QK]
    ya_s = _decode_attn(hv(q_e).astype(F32), hv(k_e), v_s.reshape(dbsz, A_HEADS, A_DV), cache_k, cache_v,
                        page_table.astype(jnp.int32), lam4, attn_g)

    y_p = _mlp(_merge(xp, ym_p, ya_p.reshape(bsz * seq, A_WIDTH), z_p, prm), prm)
    y_s = _mlp(_merge(xs, ym_s.reshape(dbsz, M_WIDTH), ya_s.reshape(dbsz, A_WIDTH), z_s, prm), prm)

    def with_meta(meta_rows, main):
        mm = jnp.broadcast_to(meta_rows[None], (bsz, N_META, A_QK))
        return jnp.concatenate([mm, main.reshape(bsz, seq, A_QK)], axis=1).reshape(bsz, N_META + seq, 1, A_HEADS, A_DV)

    av = slice(Z_AV_BLK * A_QK, (Z_AV_BLK + 1) * A_QK)
    k_prompt = with_meta(k_e[:N_META], k_p)
    v_prompt = with_meta(z_e[:N_META, av], z_p[:, av])
    conv_prompt = z_p.reshape(bsz, seq, Z_COLS)[:, seq - (CONV_W - 1):, :QK_M][None]
    return (y_p.reshape(bsz, seq, d), y_s.reshape(dbsz, 1, d),
            k_prompt, v_prompt,
            c_p[None], n_p.reshape(1, bsz, M_HEADS, M_DK), m_p.reshape(1, bsz, M_HEADS), conv_prompt,
            k_e[N_META:N_META + dbsz].reshape(dbsz, 1, 1, A_HEADS, A_DV), v_s.reshape(dbsz, 1, 1, A_HEADS, A_DV),
            c_s[None], n_s.reshape(1, dbsz, M_HEADS, M_DK), m_s.reshape(1, dbsz, M_HEADS), conv_s[None])
```

```python
import functools
import math

import jax
import jax.numpy as jnp
from jax import lax
from jax.experimental import pallas as pl
from jax.experimental.pallas import tpu as pltpu

F32 = jnp.float32
BF16 = jnp.bfloat16

D_MODEL = 1024
N_META = 16
M_HEADS = 4
M_DK = 128
M_DV = 128
M_WIDTH = M_HEADS * M_DV
QK_M = 2 * M_HEADS * M_DK
CONV_W = 4
NEG_BIG = -1e30
A_HEADS = 8
A_DH = 64
A_DV = 2 * A_DH
A_QK = A_HEADS * 2 * A_DH
A_WIDTH = A_HEADS * A_DV
ROT_DIM = A_DH // 4
ROPE_THETA = 500000.0
D_FF = 4 * D_MODEL
EPS = 1e-6
LAM_INIT = 0.8 - 0.6 * math.exp(-0.3 * 0)
Q_SCALE = (A_DH ** -0.5) * math.log2(math.e)

LANES = 128
SUBLANES = 8
VMEM_LIMIT_BYTES = 56 * 1024 * 1024

Z_QK = 0
Z_MV_BLK = 2
Z_MO_BLK = 3
Z_AQ_BLK = 2
Z_AK_BLK = 3
Z_AV_BLK = 4
Z_GA_BLK = 5
Z_GB_BLK = 6
Z_GATE_COL = 7 * 1024
Z_GATE_BLK = Z_GATE_COL // LANES
Z_COLS = Z_GATE_COL + LANES

CHUNK = 128


def _cparams(sem):
    return pltpu.CompilerParams(dimension_semantics=sem, vmem_limit_bytes=VMEM_LIMIT_BYTES)


def _pick(n, target):
    if n <= target:
        return n
    t = target
    while t >= SUBLANES:
        if n % t == 0 and t % SUBLANES == 0:
            return t
        t -= SUBLANES
    return n


def _proj_kernel(x_ref, g_ref, w_ref, o_ref, xn_ref):
    @pl.when(pl.program_id(1) == 0)
    def _():
        x = x_ref[...]
        ms = jnp.mean(x * x, axis=-1, keepdims=True)
        xn_ref[...] = (x * lax.rsqrt(ms + EPS) * g_ref[...]).astype(BF16)

    o_ref[...] = jnp.dot(xn_ref[...], w_ref[...], preferred_element_type=F32)


def _proj(x, g, w):
    m, d = x.shape
    n = w.shape[1]
    tm = _pick(m, 1024)
    tn = n // 3 if (n % (3 * LANES) == 0 and n // 3 <= 4096) else _pick(n, 2048)
    return pl.pallas_call(
        _proj_kernel,
        out_shape=jax.ShapeDtypeStruct((m, n), F32),
        grid=(m // tm, n // tn),
        in_specs=[pl.BlockSpec((tm, d), lambda i, j: (i, 0)),
                  pl.BlockSpec((1, d), lambda i, j: (0, 0)),
                  pl.BlockSpec((d, tn), lambda i, j: (0, j))],
        out_specs=pl.BlockSpec((tm, tn), lambda i, j: (i, j)),
        scratch_shapes=[pltpu.VMEM((tm, d), BF16)],
        compiler_params=_cparams(("parallel", "arbitrary")),
        name="proj",
    )(x, g, w)


def _rope_apply(x, c, sa, sb):
    reps = x.shape[1] // LANES
    c = jnp.concatenate([c] * reps, axis=1)
    sa = jnp.concatenate([sa] * reps, axis=1)
    sb = jnp.concatenate([sb] * reps, axis=1)
    half = ROT_DIM // 2
    up = pltpu.roll(x, half, axis=1)
    dn = pltpu.roll(x, x.shape[1] - half, axis=1)
    return x * c + dn * sb + up * sa


def _rope_kernel(q_ref, k_ref, v_ref, c_ref, sa_ref, sb_ref, qo_ref, ko_ref, kb_ref, vb_ref):
    c, sa, sb = c_ref[...], sa_ref[...], sb_ref[...]
    q = _rope_apply(q_ref[...], c, sa, sb)
    qo_ref[...] = (q * Q_SCALE).astype(BF16)
    k = _rope_apply(k_ref[...], c, sa, sb)
    ko_ref[...] = k
    kb_ref[...] = k.astype(BF16)
    vb_ref[...] = v_ref[...].astype(BF16)


def _rope_tables(pos):
    half = ROT_DIM // 2
    inv = ROPE_THETA ** (-jnp.arange(0, ROT_DIM, 2, dtype=F32) / ROT_DIM)
    ang = pos.astype(F32)[:, None] * inv[None, :]
    cos, sin = jnp.cos(ang), jnp.sin(ang)
    p = pos.shape[0]
    one = jnp.ones((p, A_DH - ROT_DIM), F32)
    zero = jnp.zeros((p, A_DH - ROT_DIM), F32)
    zh = jnp.zeros((p, half), F32)
    c = jnp.concatenate([cos, cos, one], axis=1)
    sa = jnp.concatenate([zh, sin, zero], axis=1)
    sb = jnp.concatenate([-sin, zh, zero], axis=1)
    tile2 = lambda a: jnp.concatenate([a, a], axis=1)
    return tile2(c), tile2(sa), tile2(sb)


def _rope(z, tables, rows_per_table):
    m = z.shape[0]
    tm = _pick(rows_per_table, 512)
    nt = rows_per_table // tm
    zspec = lambda blk: pl.BlockSpec((tm, A_QK), lambda i, blk=blk: (i, blk))
    tspec = pl.BlockSpec((tm, LANES), lambda i: (i % nt, 0))
    ospec = pl.BlockSpec((tm, A_QK), lambda i: (i, 0))
    return pl.pallas_call(
        _rope_kernel,
        out_shape=(jax.ShapeDtypeStruct((m, A_QK), BF16), jax.ShapeDtypeStruct((m, A_QK), F32),
                   jax.ShapeDtypeStruct((m, A_QK), BF16), jax.ShapeDtypeStruct((m, A_WIDTH), BF16)),
        grid=(m // tm,),
        in_specs=[zspec(Z_AQ_BLK), zspec(Z_AK_BLK), zspec(Z_AV_BLK), tspec, tspec, tspec],
        out_specs=(ospec, ospec, ospec, ospec),
        compiler_params=_cparams(("parallel",)),
        name="rope",
    )(z, z, z, *tables)


def _log_sigmoid(x):
    return -(jnp.maximum(-x, 0.0) + jnp.log1p(jnp.exp(-jnp.abs(x))))


def _sigmoid(x):
    return 1.0 / (1.0 + jnp.exp(-x))


def _dot_hilo(a_bf, x):
    hi = x.astype(BF16)
    lo = (x - hi.astype(F32)).astype(BF16)
    return (jnp.dot(a_bf, hi, preferred_element_type=F32) + jnp.dot(a_bf, lo, preferred_element_type=F32))


def _dot_hilo_r(x, a_bf):
    hi = x.astype(BF16)
    lo = (x - hi.astype(F32)).astype(BF16)
    return (jnp.dot(hi, a_bf, preferred_element_type=F32) + jnp.dot(lo, a_bf, preferred_element_type=F32))


def _mlstm_kernel(n_valid, u_ref, mv_ref, mo_ref, gc_ref, gr_ref, cinit_ref, c0_ref, n0_ref, m0_ref,
                  cw_ref, cb_ref, bc_ref, br_ref, ng_ref,
                  y_ref, co_ref, no_ref, mo_out_ref,
                  ubuf, c_s, n_s, m_s):
    L = u_ref.shape[0]
    c_id = pl.program_id(1)

    @pl.when(c_id == 0)
    def _():
        ubuf[0:SUBLANES, :] = cinit_ref[...]
        c_s[...] = c0_ref[...]
        n_s[...] = n0_ref[...]
        m_s[...] = m0_ref[...]

    ubuf[SUBLANES:SUBLANES + L, :] = u_ref[...]
    y = ubuf[SUBLANES - 3:SUBLANES - 3 + L, :] * cw_ref[0:1, :] + cb_ref[...]
    for j in range(1, CONV_W):
        y = y + ubuf[SUBLANES - 3 + j:SUBLANES - 3 + j + L, :] * cw_ref[j:j + 1, :]
    qk = y * _sigmoid(y)
    ubuf[0:SUBLANES, :] = ubuf[L:L + SUBLANES, :]

    gcol = gc_ref[...] + bc_ref[...]
    grow = gr_ref[...] + br_ref[...]
    li_c, lf_c = gcol, _log_sigmoid(gcol)
    li_r, lf_r = grow, _log_sigmoid(grow)
    if n_valid < L:
        vc = lax.broadcasted_iota(jnp.int32, gcol.shape, 0) < n_valid
        vr = lax.broadcasted_iota(jnp.int32, grow.shape, 1) < n_valid
        li_c, lf_c = jnp.where(vc, li_c, NEG_BIG), jnp.where(vc, lf_c, 0.0)
        li_r, lf_r = jnp.where(vr, li_r, NEG_BIG), jnp.where(vr, lf_r, 0.0)
    t_i = lax.broadcasted_iota(jnp.int32, (L, L), 0)
    s_i = lax.broadcasted_iota(jnp.int32, (L, L), 1)
    tril = s_i <= t_i
    tri_lo = jnp.where(tril, 1.0, 0.0).astype(BF16)
    tri_up = jnp.where(t_i <= s_i, 1.0, 0.0).astype(BF16)
    b_c = _dot_hilo(tri_lo, lf_c)
    b_r = _dot_hilo_r(lf_r, tri_up)

    for h in range(M_HEADS):
        hs = slice(h * M_DK, (h + 1) * M_DK)
        q = qk[:, hs]
        k = qk[:, M_HEADS * M_DK + h * M_DK:M_HEADS * M_DK + (h + 1) * M_DK] * (M_DK ** -0.5)
        v = mv_ref[:, hs]
        qb, kb = q.astype(BF16), k.astype(BF16)
        a_row = li_r[h:h + 1, :] - b_r[M_HEADS + h:M_HEADS + h + 1, :]
        b_col = b_c[:, M_HEADS + h:M_HEADS + h + 1]
        a_col = li_c[:, h:h + 1] - b_col
        m_prev = m_s[h]
        c_prev = c_s[h]
        n_prev = n_s[h]
        a_mat = jnp.where(tril, jnp.broadcast_to(a_row, (L, L)), -jnp.inf)
        g_col = jnp.maximum(m_prev, jnp.max(a_mat, axis=1, keepdims=True))
        d_mat = jnp.exp(a_mat - g_col)
        w_inter = jnp.exp(m_prev - g_col)
        s_mat = lax.dot_general(qb, kb, (((1,), (1,)), ((), ())), preferred_element_type=F32) * d_mat
        cq = lax.dot_general(qb, c_prev.astype(BF16), (((1,), (1,)), ((), ())), preferred_element_type=F32)
        num = jnp.dot(s_mat.astype(BF16), v.astype(BF16), preferred_element_type=F32) + w_inter * cq
        den = jnp.sum(s_mat, axis=1, keepdims=True) + w_inter * jnp.sum(q * n_prev, axis=1, keepdims=True)
        m_t = b_col + g_col
        hh = num / jnp.maximum(jnp.abs(den), jnp.exp(-m_t))
        g_last = g_col[L - 1:L, :]
        w_col = jnp.exp(a_col - g_last)
        decay = jnp.exp(m_prev - g_last)
        vw = (v * w_col).astype(BF16)
        c_s[h] = decay * c_prev + lax.dot_general(vw, kb, (((0,), (0,)), ((), ())), preferred_element_type=F32)
        n_s[h] = decay * n_prev + jnp.sum(k * w_col, axis=0, keepdims=True)
        m_s[h] = b_col[L - 1:L, :] + g_last
        hn = hh * lax.rsqrt(jnp.mean(hh * hh, axis=-1, keepdims=True) + EPS)
        y_ref[:, hs] = (hn * ng_ref[:, hs] * _sigmoid(mo_ref[:, hs])).astype(y_ref.dtype)

    @pl.when(c_id == pl.num_programs(1) - 1)
    def _():
        co_ref[...] = c_s[...]
        no_ref[...] = n_s[...]
        mo_out_ref[...] = m_s[...]


def _mlstm(z, gates_t, conv_init, c0, n0, m0, prm, nb, nc, n_valid):
    L = CHUNK
    rows = nb * nc * L
    bsel = (lambda b: b) if c0.shape[0] == nb else (lambda b: 0)
    csel = (lambda b: b) if conv_init.shape[0] == nb else (lambda b: 0)
    row = lambda b, c: b * nc + c
    const = lambda shape: pl.BlockSpec(shape, lambda b, c: (0,) * len(shape))
    in_specs = [
        pl.BlockSpec((L, QK_M), lambda b, c: (row(b, c), 0)),
        pl.BlockSpec((L, M_WIDTH), lambda b, c: (row(b, c), Z_MV_BLK)),
        pl.BlockSpec((L, M_WIDTH), lambda b, c: (row(b, c), Z_MO_BLK)),
        pl.BlockSpec((L, LANES), lambda b, c: (row(b, c), Z_GATE_BLK)),
        pl.BlockSpec((SUBLANES, L), lambda b, c: (0, row(b, c))),
        pl.BlockSpec((None, SUBLANES, QK_M), lambda b, c: (csel(b), 0, 0)),
        pl.BlockSpec((None, M_HEADS, M_DV, M_DK), lambda b, c: (bsel(b), 0, 0, 0)),
        pl.BlockSpec((None, M_HEADS, 1, M_DK), lambda b, c: (bsel(b), 0, 0, 0)),
        pl.BlockSpec((None, M_HEADS, 1, 1), lambda b, c: (bsel(b), 0, 0, 0)),
        const((CONV_W, QK_M)), const((1, QK_M)), const((1, LANES)), const((SUBLANES, L)), const((1, M_WIDTH)),
    ]
    out_specs = (
        pl.BlockSpec((L, M_WIDTH), lambda b, c: (row(b, c), 0)),
        pl.BlockSpec((None, M_HEADS, M_DV, M_DK), lambda b, c: (b, 0, 0, 0)),
        pl.BlockSpec((None, M_HEADS, 1, M_DK), lambda b, c: (b, 0, 0, 0)),
        pl.BlockSpec((None, M_HEADS, 1, 1), lambda b, c: (b, 0, 0, 0)),
    )
    out_shape = (
        jax.ShapeDtypeStruct((rows, M_WIDTH), BF16),
        jax.ShapeDtypeStruct((nb, M_HEADS, M_DV, M_DK), F32),
        jax.ShapeDtypeStruct((nb, M_HEADS, 1, M_DK), F32),
        jax.ShapeDtypeStruct((nb, M_HEADS, 1, 1), F32),
    )
    return pl.pallas_call(
        functools.partial(_mlstm_kernel, n_valid),
        out_shape=out_shape,
        grid=(nb, nc),
        in_specs=in_specs,
        out_specs=out_specs,
        scratch_shapes=[pltpu.VMEM((L + SUBLANES, QK_M), F32), pltpu.VMEM((M_HEADS, M_DV, M_DK), F32),
                        pltpu.VMEM((M_HEADS, 1, M_DK), F32), pltpu.VMEM((M_HEADS, 1, 1), F32)],
        compiler_params=_cparams(("parallel", "arbitrary")),
        name="mlstm",
    )(z, z, z, z, gates_t, conv_init, c0, n0, m0,
      prm["conv_w"], prm["conv_b"], prm["b_if_row"], prm["b_if_col"], prm["mlstm_g"])


def _mstep_kernel(u_ref, mv_ref, mo_ref, g_ref, sc_ref, c0_ref, n0_ref, m0_ref,
                  cw_ref, cb_ref, bif_ref, ng_ref,
                  y_ref, co_ref, no_ref, mo_out_ref, conv_ref):
    u = u_ref[...]
    sc = sc_ref[...]
    y = sc[0:1, :] * cw_ref[0:1, :] + cb_ref[...]
    y = y + sc[1:2, :] * cw_ref[1:2, :]
    y = y + sc[2:3, :] * cw_ref[2:3, :]
    y = y + u * cw_ref[3:4, :]
    qk = y * _sigmoid(y)
    conv_ref[0:2, :] = sc[1:3, :]
    conv_ref[2:3, :] = u
    gates = g_ref[...] + bif_ref[...]
    lf_all = _log_sigmoid(gates)
    e_i = lax.broadcasted_iota(jnp.int32, (M_DV, M_DV), 0)
    e_j = lax.broadcasted_iota(jnp.int32, (M_DV, M_DV), 1)
    eye = e_i == e_j
    for h in range(M_HEADS):
        hs = slice(h * M_DK, (h + 1) * M_DK)
        q = qk[:, hs]
        k = qk[:, M_HEADS * M_DK + h * M_DK:M_HEADS * M_DK + (h + 1) * M_DK] * (M_DK ** -0.5)
        v = mv_ref[:, hs]
        li = gates[:, h:h + 1]
        lf = lf_all[:, M_HEADS + h:M_HEADS + h + 1]
        m_prev = m0_ref[h]
        c_prev = c0_ref[h]
        n_prev = n0_ref[h]
        m_t = jnp.maximum(lf + m_prev, li)
        d_w = jnp.exp(li - m_t)
        w_inter = jnp.exp(lf + m_prev - m_t)
        s = jnp.sum(q * k, axis=1, keepdims=True) * d_w
        q8 = jnp.broadcast_to(q, (SUBLANES, M_DK)).astype(BF16)
        cq = lax.dot_general(q8, c_prev.astype(BF16), (((1,), (1,)), ((), ())), preferred_element_type=F32)[0:1, :]
        num = s * v + w_inter * cq
        den = s + w_inter * jnp.sum(q * n_prev, axis=1, keepdims=True)
        hh = num / jnp.maximum(jnp.abs(den), jnp.exp(-m_t))
        vdiag = jnp.where(eye, jnp.broadcast_to(v * d_w, (M_DV, M_DV)), 0.0).astype(BF16)
        krows = jnp.broadcast_to(k, (M_DV, M_DK)).astype(BF16)
        co_ref[h] = w_inter * c_prev + jnp.dot(vdiag, krows, preferred_element_type=F32)
        no_ref[h] = w_inter * n_prev + d_w * k
        mo_out_ref[h] = m_t
        hn = hh * lax.rsqrt(jnp.mean(hh * hh, axis=-1, keepdims=True) + EPS)
        y_ref[:, hs] = (hn * ng_ref[:, hs] * _sigmoid(mo_ref[:, hs])).astype(y_ref.dtype)


def _mstep(z3, state_conv, c0, n0, m0, prm):
    nb = z3.shape[0]
    const = lambda shape: pl.BlockSpec(shape, lambda b: (0,) * len(shape))
    zspec = lambda w, blk: pl.BlockSpec((None, 1, w), lambda b, blk=blk: (b, 0, blk))
    st = lambda shape: pl.BlockSpec((None,) + shape, lambda b: (b,) + (0,) * len(shape))
    return pl.pallas_call(
        _mstep_kernel,
        out_shape=(jax.ShapeDtypeStruct((nb, 1, M_WIDTH), BF16),
                   jax.ShapeDtypeStruct((nb, M_HEADS, M_DV, M_DK), F32),
                   jax.ShapeDtypeStruct((nb, M_HEADS, 1, M_DK), F32),
                   jax.ShapeDtypeStruct((nb, M_HEADS, 1, 1), F32),
                   jax.ShapeDtypeStruct((nb, CONV_W - 1, QK_M), F32)),
        grid=(nb,),
        in_specs=[zspec(QK_M, 0), zspec(M_WIDTH, Z_MV_BLK), zspec(M_WIDTH, Z_MO_BLK), zspec(LANES, Z_GATE_BLK),
                  st((CONV_W - 1, QK_M)), st((M_HEADS, M_DV, M_DK)), st((M_HEADS, 1, M_DK)), st((M_HEADS, 1, 1)),
                  const((CONV_W, QK_M)), const((1, QK_M)), const((1, LANES)), const((1, M_WIDTH))],
        out_specs=(pl.BlockSpec((None, 1, M_WIDTH), lambda b: (b, 0, 0)),
                   st((M_HEADS, M_DV, M_DK)), st((M_HEADS, 1, M_DK)), st((M_HEADS, 1, 1)), st((CONV_W - 1, QK_M))),
        compiler_params=_cparams(("parallel",)),
        name="mstep",
    )(z3, z3, z3, z3, state_conv, c0, n0, m0,
      prm["conv_w"], prm["conv_b"], prm["b_if_row"], prm["mlstm_g"])


def _diff_lambda(l_ref):
    a = jnp.sum(l_ref[0:1, :] * l_ref[1:2, :], axis=1, keepdims=True)
    b = jnp.sum(l_ref[2:3, :] * l_ref[3:4, :], axis=1, keepdims=True)
    return jnp.exp(a) - jnp.exp(b) + LAM_INIT


def _attn_kernel(n_meta, rb, q_ref, k_ref, v_ref, km_ref, vm_ref, lam_ref, ag_ref, o_ref, qq_s, mx_s, acc_s, s_s):
    tq = q_ref.shape[0]
    i = pl.program_id(2)
    q = q_ref[...]
    lane = lax.broadcasted_iota(jnp.int32, q.shape, 1)
    zero = jnp.zeros_like(q)
    qq_s[0:tq, :] = jnp.where(lane < A_DH, q, zero)
    qq_s[tq:2 * tq, :] = jnp.where(lane >= A_DH, q, zero)
    nt = (((1,), (1,)), ((), ()))
    blocks = [(r0, r0 % tq) for r0 in range(0, 2 * tq, rb)]

    def scores(r0, kc):
        return lax.dot_general(qq_s[r0:r0 + rb, :], kc, nt, preferred_element_type=F32)

    def meta_scores(r0):
        s = scores(r0, km_ref[...])
        col = lax.broadcasted_iota(jnp.int32, s.shape, 1)
        return jnp.where(col < n_meta, s, -jnp.inf)

    def diag_scores(r0, q0, start):
        s = scores(r0, k_ref[pl.ds(start, q0 + rb), :])
        r = lax.broadcasted_iota(jnp.int32, s.shape, 0) + q0
        c = lax.broadcasted_iota(jnp.int32, s.shape, 1)
        return jnp.where(c <= r, s, -jnp.inf)

    def fold_max(m, s):
        for t in range(s.shape[1] // LANES):
            m = jnp.maximum(m, s[:, t * LANES:(t + 1) * LANES])
        return m

    def probs(s, m_rep):
        return jnp.exp2(s - jnp.concatenate([m_rep] * (s.shape[1] // LANES), axis=1)).astype(BF16)

    def with_ones(v):
        return jnp.concatenate([v, jnp.ones((v.shape[0], LANES), v.dtype)], axis=1)

    dstart = pl.multiple_of(i * tq, tq)

    for r0, _ in blocks:
        mx_s[r0:r0 + rb, :] = meta_scores(r0)

    def pass1(j, carry):
        kc = k_ref[pl.ds(pl.multiple_of(j * tq, tq), tq), :]
        for r0, _ in blocks:
            s = scores(r0, kc)
            s_s[j, r0:r0 + rb, :] = s
            mx_s[r0:r0 + rb, :] = fold_max(mx_s[r0:r0 + rb, :], s)
        return carry

    lax.fori_loop(0, i, pass1, 0)
    for r0, q0 in blocks:
        s = diag_scores(r0, q0, dstart)
        s_s[i, r0:r0 + rb, 0:q0 + rb] = s
        m = fold_max(mx_s[r0:r0 + rb, :], s)
        mx_s[r0:r0 + rb, :] = jnp.broadcast_to(jnp.max(m, axis=1, keepdims=True), (rb, LANES))

    vm1 = with_ones(vm_ref[...])
    for r0, _ in blocks:
        acc_s[r0:r0 + rb, :] = jnp.dot(probs(meta_scores(r0), mx_s[r0:r0 + rb, :]), vm1, preferred_element_type=F32)

    def pass2(j, carry):
        v1 = with_ones(v_ref[pl.ds(pl.multiple_of(j * tq, tq), tq), :])
        for r0, _ in blocks:
            p = probs(s_s[j, r0:r0 + rb, :], mx_s[r0:r0 + rb, :])
            acc_s[r0:r0 + rb, :] += jnp.dot(p, v1, preferred_element_type=F32)
        return carry

    lax.fori_loop(0, i, pass2, 0)
    for r0, q0 in blocks:
        p = probs(s_s[i, r0:r0 + rb, 0:q0 + rb], mx_s[r0:r0 + rb, :])
        v1 = with_ones(v_ref[pl.ds(dstart, q0 + rb), :])
        acc_s[r0:r0 + rb, :] += jnp.dot(p, v1, preferred_element_type=F32)

    lam = _diff_lambda(lam_ref)
    o0 = acc_s[0:tq, 0:A_DV] / acc_s[0:tq, A_DV:2 * A_DV]
    o1 = acc_s[tq:2 * tq, 0:A_DV] / acc_s[tq:2 * tq, A_DV:2 * A_DV]
    o = o0 - lam * o1
    on = o * lax.rsqrt(jnp.mean(o * o, axis=-1, keepdims=True) + EPS)
    o_ref[...] = (on * ag_ref[...] * (1.0 - LAM_INIT)).astype(o_ref.dtype)


def _attn(q, k, v, k_meta, v_meta, lam4, attn_g, n_meta):
    bsz, t, _ = q.shape
    tq = _pick(t, 512)
    rb = _pick(tq, 256)
    hspec = lambda rows: pl.BlockSpec((None, rows, LANES), lambda b, h, i: (b, 0, h))
    return pl.pallas_call(
        functools.partial(_attn_kernel, n_meta, rb),
        out_shape=jax.ShapeDtypeStruct((bsz, t, A_WIDTH), BF16),
        grid=(bsz, A_HEADS, t // tq),
        in_specs=[pl.BlockSpec((None, tq, LANES), lambda b, h, i: (b, i, h)),
                  hspec(t), hspec(t),
                  pl.BlockSpec((LANES, LANES), lambda b, h, i: (0, h)),
                  pl.BlockSpec((LANES, LANES), lambda b, h, i: (0, h)),
                  pl.BlockSpec((4, A_DH), lambda b, h, i: (0, 0)),
                  pl.BlockSpec((1, A_DV), lambda b, h, i: (0, 0))],
        out_specs=pl.BlockSpec((None, tq, LANES), lambda b, h, i: (b, i, h)),
        scratch_shapes=[pltpu.VMEM((2 * tq, LANES), BF16), pltpu.VMEM((2 * tq, LANES), F32),
                        pltpu.VMEM((2 * tq, 2 * A_DV), F32), pltpu.VMEM((t // tq, 2 * tq, tq), F32)],
        compiler_params=_cparams(("parallel", "parallel", "arbitrary")),
        name="attn",
    )(q, k, v, k_meta, v_meta, lam4, attn_g)


def _class_allreduce(x, op):
    for sh in (A_HEADS, 2 * A_HEADS, 4 * A_HEADS, 8 * A_HEADS):
        x = op(x, pltpu.roll(x, sh, axis=1))
    return x


def _decode_kernel(g_pages, *refs):
    pt_ref = refs[0]
    del pt_ref
    k_refs = refs[1:1 + g_pages]
    v_refs = refs[1 + g_pages:1 + 2 * g_pages]
    q_ref, kn_ref, vn_ref, lam_ref, ag_ref, o_ref, m_s, l_s, acc_s = refs[1 + 2 * g_pages:]
    p_id = pl.program_id(1)
    nt = (((1,), (1,)), ((), ()))
    q = q_ref[...]
    r_i = lax.broadcasted_iota(jnp.int32, (SUBLANES, LANES), 0)
    l_i = lax.broadcasted_iota(jnp.int32, (SUBLANES, LANES), 1)
    rsel = jnp.where((r_i < 2) & ((l_i >= A_DH) == (r_i == 1)), 1.0, 0.0).astype(BF16)
    diag = (l_i % A_HEADS) == r_i

    def colvec(x_rep):
        cols = []
        for c in range(2):
            xb = jnp.broadcast_to(x_rep[c:c + 1, :], (SUBLANES, LANES))
            cols.append(jnp.sum(jnp.where(diag & (l_i < A_HEADS), xb, 0.0), axis=1, keepdims=True))
        return jnp.concatenate(cols, axis=0)

    def block(kmat, vmat, n_lanes_valid):
        rows = kmat.shape[0]
        st = lax.dot_general(rsel, kmat.astype(BF16), nt, preferred_element_type=F32)
        nv = rows // LANES
        tiles = [st[:, t * LANES:(t + 1) * LANES] for t in range(nv)]
        if n_lanes_valid is not None:
            tiles = [jnp.where(l_i + t * LANES < n_lanes_valid, x, -jnp.inf) for t, x in enumerate(tiles)]
        mb = tiles[0]
        for x in tiles[1:]:
            mb = jnp.maximum(mb, x)
        mb = _class_allreduce(mb, jnp.maximum)
        m_old = m_s[...]
        m_new = jnp.maximum(m_old, mb)
        alpha = jnp.exp2(m_old - m_new)
        ps = [jnp.exp2(x - m_new) for x in tiles]
        lsum = ps[0]
        for x in ps[1:]:
            lsum = lsum + x
        l_s[...] = alpha * l_s[...] + lsum
        m_s[...] = m_new
        pm = []
        for c in range(2):
            pm.append(jnp.concatenate(
                [jnp.where(diag, jnp.broadcast_to(x[c:c + 1, :], (SUBLANES, LANES)), 0.0) for x in ps], axis=1))
        pm = jnp.concatenate(pm, axis=0).astype(BF16)
        pv = jnp.dot(pm, vmat.astype(BF16), preferred_element_type=F32)
        acc_s[...] = colvec(alpha) * acc_s[...] + pv

    @pl.when(p_id == 0)
    def _():
        m_s[...] = jnp.full(m_s.shape, -jnp.inf, F32)
        l_s[...] = jnp.zeros(l_s.shape, F32)
        acc_s[...] = jnp.zeros(acc_s.shape, F32)
        pad = jnp.zeros((LANES - SUBLANES, LANES), F32)
        m_s[...] = jnp.full(m_s.shape, NEG_BIG, F32)
        block(jnp.concatenate([kn_ref[...] * q, pad], axis=0), jnp.concatenate([vn_ref[...], pad], axis=0), A_HEADS)

    kmat = jnp.concatenate([(r[...] * q[None, :, :]).reshape(-1, LANES) for r in k_refs], axis=0)
    vmat = jnp.concatenate([r[...].reshape(-1, LANES) for r in v_refs], axis=0)
    block(kmat, vmat, None)

    @pl.when(p_id == pl.num_programs(1) - 1)
    def _():
        lam = _diff_lambda(lam_ref)
        l_rep = _class_allreduce(l_s[...], jnp.add)
        o = acc_s[...] / colvec(l_rep)
        o = o[0:A_HEADS, :] - lam * o[A_HEADS:2 * A_HEADS, :]
        on = o * lax.rsqrt(jnp.mean(o * o, axis=-1, keepdims=True) + EPS)
        o_ref[...] = (on * ag_ref[...] * (1.0 - LAM_INIT)).astype(o_ref.dtype)


def _decode_attn(q, k_new, v_new, cache_k, cache_v, page_table, lam4, attn_g):
    db, n_pages = page_table.shape
    page = cache_k.shape[1]
    g_pages = 8 if n_pages % 8 == 0 else (2 if n_pages % 2 == 0 else 1)

    def pspec(g):
        return pl.BlockSpec((None, page, None, A_HEADS, A_DV),
                            lambda b, p, pt, g=g: (pt[b, p * g_pages + g], 0, 0, 0, 0))

    row = pl.BlockSpec((None, A_HEADS, A_DV), lambda b, p, pt: (b, 0, 0))
    grid_spec = pltpu.PrefetchScalarGridSpec(
        num_scalar_prefetch=1,
        grid=(db, n_pages // g_pages),
        in_specs=[pspec(g) for g in range(g_pages)] + [pspec(g) for g in range(g_pages)] + [
            row, row, row,
            pl.BlockSpec((4, A_DH), lambda b, p, pt: (0, 0)),
            pl.BlockSpec((1, A_DV), lambda b, p, pt: (0, 0))],
        out_specs=row,
        scratch_shapes=[pltpu.VMEM((SUBLANES, LANES), F32), pltpu.VMEM((SUBLANES, LANES), F32),
                        pltpu.VMEM((2 * A_HEADS, A_DV), F32)],
    )
    return pl.pallas_call(
        functools.partial(_decode_kernel, g_pages),
        out_shape=jax.ShapeDtypeStruct((db, A_HEADS, A_DV), BF16),
        grid_spec=grid_spec,
        compiler_params=_cparams(("parallel", "arbitrary")),
        name="decode",
    )(page_table, *([cache_k] * g_pages), *([cache_v] * g_pages), q, k_new, v_new, lam4, attn_g)


def _merge_kernel(x_ref, ym_ref, ya_ref, ga_ref, gb_ref, wpa_ref, wpb_ref, wo_ref, o_ref):
    pa = jnp.dot(ym_ref[...], wpa_ref[...], preferred_element_type=F32)
    pb = jnp.dot(ya_ref[...], wpb_ref[...], preferred_element_type=F32)
    mixed = _sigmoid(ga_ref[...]) * pa + _sigmoid(gb_ref[...]) * pb
    o_ref[...] = x_ref[...] + jnp.dot(mixed.astype(BF16), wo_ref[...], preferred_element_type=F32)


def _merge(x, y_m, y_a, z, prm):
    m = x.shape[0]
    tm = _pick(m, 512)
    rspec = lambda w, blk=0: pl.BlockSpec((tm, w), lambda i, blk=blk: (i, blk))
    const = lambda shape: pl.BlockSpec(shape, lambda i: (0, 0))
    return pl.pallas_call(
        _merge_kernel,
        out_shape=jax.ShapeDtypeStruct((m, D_MODEL), F32),
        grid=(m // tm,),
        in_specs=[rspec(D_MODEL), rspec(M_WIDTH), rspec(A_WIDTH), rspec(D_MODEL, Z_GA_BLK), rspec(D_MODEL, Z_GB_BLK),
                  const((M_WIDTH, D_MODEL)), const((A_WIDTH, D_MODEL)), const((D_MODEL, D_MODEL))],
        out_specs=rspec(D_MODEL),
        compiler_params=_cparams(("parallel",)),
        name="merge",
    )(x, y_m, y_a, z, z, prm["w_pa"], prm["w_pb"], prm["w_out"])


def _mlp_kernel(x_ref, g1_ref, wu_ref, wd_ref, g2_ref, o_ref):
    x = x_ref[...]
    xn = (x * lax.rsqrt(jnp.mean(x * x, axis=-1, keepdims=True) + EPS) * g1_ref[...]).astype(BF16)
    acc = x
    fc = D_MODEL
    for f in range(D_FF // fc):
        hf = jnp.dot(xn, wu_ref[:, f * fc:(f + 1) * fc], preferred_element_type=F32)
        hf = jnp.square(jnp.maximum(hf, 0.0)).astype(BF16)
        acc = acc + jnp.dot(hf, wd_ref[f * fc:(f + 1) * fc, :], preferred_element_type=F32)
    o_ref[...] = acc * lax.rsqrt(jnp.mean(acc * acc, axis=-1, keepdims=True) + EPS) * g2_ref[...]


def _mlp(x, prm):
    m = x.shape[0]
    tm = _pick(m, 512)
    const = lambda shape: pl.BlockSpec(shape, lambda i: (0, 0))
    return pl.pallas_call(
        _mlp_kernel,
        out_shape=jax.ShapeDtypeStruct((m, D_MODEL), F32),
        grid=(m // tm,),
        in_specs=[pl.BlockSpec((tm, D_MODEL), lambda i: (i, 0)), const((1, D_MODEL)),
                  const((D_MODEL, D_FF)), const((D_FF, D_MODEL)), const((1, D_MODEL))],
        out_specs=pl.BlockSpec((tm, D_MODEL), lambda i: (i, 0)),
        compiler_params=_cparams(("parallel",)),
        name="mlp",
    )(x, prm["ffn_g"], prm["w_up"], prm["w_down"], prm["final_g"])


def kernel(x_prompt, x_sample, cache_k, cache_v, state_C, state_n, state_m, state_conv, page_table, meta_tokens, norm_mix_g, norm_ffn_g, w_in, b_if, conv_w, conv_b, mlstm_norm_g, lambda_q1, lambda_k1, lambda_q2, lambda_k2, attn_norm_g, w_proj_a, w_proj_b, w_out, w_up, w_down, final_norm_g):
    bsz, seq, d = x_prompt.shape
    dbsz, s_dec, _ = x_sample.shape
    n_pages = page_table.shape[1]
    past = n_pages * cache_k.shape[1]
    assert d == D_MODEL and s_dec == 1 and seq % CHUNK == 0 and norm_mix_g.shape[0] == 1
    assert N_META + dbsz <= CHUNK
    l = 0

    gate_lo = QK_M + 2 * M_WIDTH
    gate_hi = gate_lo + 2 * M_HEADS
    w = w_in[l]
    w_re = jnp.concatenate([w[:, :gate_lo], w[:, gate_hi:], w[:, gate_lo:gate_hi],
                            jnp.zeros((d, LANES - 2 * M_HEADS), w.dtype)], axis=1).astype(BF16)
    bif = b_if[l].astype(F32)
    prm = {
        "conv_w": conv_w[l].astype(F32), "conv_b": conv_b[l].astype(F32)[None, :],
        "b_if_row": jnp.pad(bif, (0, LANES - 2 * M_HEADS))[None, :],
        "b_if_col": jnp.broadcast_to(bif[:, None], (2 * M_HEADS, CHUNK)),
        "mlstm_g": mlstm_norm_g[l].astype(F32)[None, :],
        "w_pa": w_proj_a[l].astype(BF16), "w_pb": w_proj_b[l].astype(BF16), "w_out": w_out[l].astype(BF16),
        "ffn_g": norm_ffn_g[l].astype(F32)[None, :], "w_up": w_up[l].astype(BF16), "w_down": w_down[l].astype(BF16),
        "final_g": final_norm_g.astype(F32)[None, :],
    }
    mix_g = norm_mix_g[l].astype(F32)[None, :]
    lam4 = jnp.stack([lambda_q1[l], lambda_k1[l], lambda_q2[l], lambda_k2[l]]).astype(F32)
    attn_g = attn_norm_g[l].astype(F32)[None, :]

    xp = x_prompt.reshape(bsz * seq, d)
    xs = x_sample.reshape(dbsz, d)
    xe = jnp.concatenate([meta_tokens.astype(F32), xs, jnp.zeros((CHUNK - N_META - dbsz, d), F32)], axis=0)
    z_p = _proj(xp, mix_g, w_re)
    z_e = _proj(xe, mix_g, w_re)

    pos_e = jnp.concatenate([jnp.arange(N_META), jnp.full((dbsz,), past), jnp.zeros((CHUNK - N_META - dbsz,), jnp.int32)])
    q_p, k_p, kb_p, vb_p = _rope(z_p, _rope_tables(N_META + jnp.arange(seq)), seq)
    q_e, k_e, kb_e, vb_e = _rope(z_e, _rope_tables(pos_e), CHUNK)

    zero_state = (jnp.zeros((1, M_HEADS, M_DV, M_DK), F32), jnp.zeros((1, M_HEADS, 1, M_DK), F32),
                  jnp.zeros((1, M_HEADS, 1, 1), F32))
    gt_e = z_e[:, Z_GATE_COL:Z_GATE_COL + 2 * M_HEADS].T
    gt_p = z_p[:, Z_GATE_COL:Z_GATE_COL + 2 * M_HEADS].T
    _, c_m, n_m, m_m = _mlstm(z_e, gt_e, jnp.zeros((1, SUBLANES, QK_M), F32), *zero_state, prm, 1, 1, N_META)
    conv_init = z_e[N_META - SUBLANES:N_META, :QK_M][None]
    ym_p, c_p, n_p, m_p = _mlstm(z_p, gt_p, conv_init, c_m, n_m, m_m, prm, bsz, seq // CHUNK, CHUNK)
    z_s = z_e[N_META:N_META + dbsz]
    ym_s, c_s, n_s, m_s, conv_s = _mstep(
        z_s[:, None, :], state_conv[l].astype(F32), state_C[l].astype(F32),
        state_n[l].astype(F32)[:, :, None, :], state_m[l].astype(F32)[:, :, None, None], prm)

    ya_p = _attn(q_p.reshape(bsz, seq, A_QK), kb_p.reshape(bsz, seq, A_QK), vb_p.reshape(bsz, seq, A_WIDTH),
                 kb_e, vb_e, lam4, attn_g, N_META)
    hv = lambda a: a[N_META:N_META + dbsz].reshape(dbsz, A_HEADS, A_DV)
    v_s = z_s[:, Z_AV_BLK * A_QK:(Z_AV_BLK + 1) * A_QK]
    ya_s = _decode_attn(hv(q_e).astype(F32), hv(k_e), v_s.reshape(dbsz, A_HEADS, A_DV), cache_k, cache_v,
                        page_table.astype(jnp.int32), lam4, attn_g)

    y_p = _mlp(_merge(xp, ym_p, ya_p.reshape(bsz * seq, A_WIDTH), z_p, prm), prm)
    y_s = _mlp(_merge(xs, ym_s.reshape(dbsz, M_WIDTH), ya_s.reshape(dbsz, A_WIDTH), z_s, prm), prm)

    def with_meta(meta_rows, main):
        mm = jnp.broadcast_to(meta_rows[None], (bsz, N_META, A_QK))
        return jnp.concatenate([mm, main.reshape(bsz, seq, A_QK)], axis=1).reshape(bsz, N_META + seq, 1, A_HEADS, A_DV)

    av = slice(Z_AV_BLK * A_QK, (Z_AV_BLK + 1) * A_QK)
    k_prompt = with_meta(k_e[:N_META], k_p)
    v_prompt = with_meta(z_e[:N_META, av], z_p[:, av])
    conv_prompt = z_p.reshape(bsz, seq, Z_COLS)[:, seq - (CONV_W - 1):, :QK_M][None]
    return (y_p.reshape(bsz, seq, d), y_s.reshape(dbsz, 1, d),
            k_prompt, v_prompt,
            c_p[None], n_p.reshape(1, bsz, M_HEADS, M_DK), m_p.reshape(1, bsz, M_HEADS), conv_prompt,
            k_e[N_META:N_META + dbsz].reshape(dbsz, 1, 1, A_HEADS, A_DV), v_s.reshape(dbsz, 1, 1, A_HEADS, A_DV),
            c_s[None], n_s.reshape(1, dbsz, M_HEADS, M_DK), m_s.reshape(1, dbsz, M_HEADS), conv_s[None])
```

```python
import functools
import math

import jax
import jax.numpy as jnp
from jax import lax
from jax.experimental import pallas as pl
from jax.experimental.pallas import tpu as pltpu

F32 = jnp.float32
BF16 = jnp.bfloat16

D_MODEL = 1024
N_META = 16
M_HEADS = 4
M_DK = 128
M_DV = 128
M_WIDTH = M_HEADS * M_DV
QK_M = 2 * M_HEADS * M_DK
CONV_W = 4
NEG_BIG = -1e30
A_HEADS = 8
A_DH = 64
A_DV = 2 * A_DH
A_QK = A_HEADS * 2 * A_DH
A_WIDTH = A_HEADS * A_DV
ROT_DIM = A_DH // 4
ROPE_THETA = 500000.0
D_FF = 4 * D_MODEL
EPS = 1e-6
LAM_INIT = 0.8 - 0.6 * math.exp(-0.3 * 0)
Q_SCALE = (A_DH ** -0.5) * math.log2(math.e)

LANES = 128
SUBLANES = 8
VMEM_LIMIT_BYTES = 56 * 1024 * 1024

Z_QK = 0
Z_MV_BLK = 2
Z_MO_BLK = 3
Z_AQ_BLK = 2
Z_AK_BLK = 3
Z_AV_BLK = 4
Z_GA_BLK = 5
Z_GB_BLK = 6
Z_GATE_COL = 7 * 1024
Z_GATE_BLK = Z_GATE_COL // LANES
Z_COLS = Z_GATE_COL + LANES

CHUNK = 128


def _cparams(sem):
    return pltpu.CompilerParams(dimension_semantics=sem, vmem_limit_bytes=VMEM_LIMIT_BYTES)


def _pick(n, target):
    if n <= target:
        return n
    t = target
    while t >= SUBLANES:
        if n % t == 0 and t % SUBLANES == 0:
            return t
        t -= SUBLANES
    return n


def _proj_kernel(x_ref, g_ref, w_ref, o_ref, xn_ref):
    @pl.when(pl.program_id(1) == 0)
    def _():
        x = x_ref[...]
        ms = jnp.mean(x * x, axis=-1, keepdims=True)
        xn_ref[...] = (x * lax.rsqrt(ms + EPS) * g_ref[...]).astype(BF16)

    o_ref[...] = jnp.dot(xn_ref[...], w_ref[...], preferred_element_type=F32)


def _proj(x, g, w):
    m, d = x.shape
    n = w.shape[1]
    tm = _pick(m, 1024)
    tn = n // 3 if (n % (3 * LANES) == 0 and n // 3 <= 4096) else _pick(n, 2048)
    return pl.pallas_call(
        _proj_kernel,
        out_shape=jax.ShapeDtypeStruct((m, n), F32),
        grid=(m // tm, n // tn),
        in_specs=[pl.BlockSpec((tm, d), lambda i, j: (i, 0)),
                  pl.BlockSpec((1, d), lambda i, j: (0, 0)),
                  pl.BlockSpec((d, tn), lambda i, j: (0, j))],
        out_specs=pl.BlockSpec((tm, tn), lambda i, j: (i, j)),
        scratch_shapes=[pltpu.VMEM((tm, d), BF16)],
        compiler_params=_cparams(("parallel", "arbitrary")),
        name="proj",
    )(x, g, w)


def _rope_apply(x, c, sa, sb):
    reps = x.shape[1] // LANES
    c = jnp.concatenate([c] * reps, axis=1)
    sa = jnp.concatenate([sa] * reps, axis=1)
    sb = jnp.concatenate([sb] * reps, axis=1)
    half = ROT_DIM // 2
    up = pltpu.roll(x, half, axis=1)
    dn = pltpu.roll(x, x.shape[1] - half, axis=1)
    return x * c + dn * sb + up * sa


def _rope_kernel(q_ref, k_ref, v_ref, c_ref, sa_ref, sb_ref, qo_ref, ko_ref, kb_ref, vb_ref):
    c, sa, sb = c_ref[...], sa_ref[...], sb_ref[...]
    q = _rope_apply(q_ref[...], c, sa, sb)
    qo_ref[...] = (q * Q_SCALE).astype(BF16)
    k = _rope_apply(k_ref[...], c, sa, sb)
    ko_ref[...] = k
    kb_ref[...] = k.astype(BF16)
    vb_ref[...] = v_ref[...].astype(BF16)


def _rope_tables(pos):
    half = ROT_DIM // 2
    inv = ROPE_THETA ** (-jnp.arange(0, ROT_DIM, 2, dtype=F32) / ROT_DIM)
    ang = pos.astype(F32)[:, None] * inv[None, :]
    cos, sin = jnp.cos(ang), jnp.sin(ang)
    p = pos.shape[0]
    one = jnp.ones((p, A_DH - ROT_DIM), F32)
    zero = jnp.zeros((p, A_DH - ROT_DIM), F32)
    zh = jnp.zeros((p, half), F32)
    c = jnp.concatenate([cos, cos, one], axis=1)
    sa = jnp.concatenate([zh, sin, zero], axis=1)
    sb = jnp.concatenate([-sin, zh, zero], axis=1)
    tile2 = lambda a: jnp.concatenate([a, a], axis=1)
    return tile2(c), tile2(sa), tile2(sb)


def _rope(z, tables, rows_per_table):
    m = z.shape[0]
    tm = _pick(rows_per_table, 512)
    nt = rows_per_table // tm
    zspec = lambda blk: pl.BlockSpec((tm, A_QK), lambda i, blk=blk: (i, blk))
    tspec = pl.BlockSpec((tm, LANES), lambda i: (i % nt, 0))
    ospec = pl.BlockSpec((tm, A_QK), lambda i: (i, 0))
    return pl.pallas_call(
        _rope_kernel,
        out_shape=(jax.ShapeDtypeStruct((m, A_QK), BF16), jax.ShapeDtypeStruct((m, A_QK), F32),
                   jax.ShapeDtypeStruct((m, A_QK), BF16), jax.ShapeDtypeStruct((m, A_WIDTH), BF16)),
        grid=(m // tm,),
        in_specs=[zspec(Z_AQ_BLK), zspec(Z_AK_BLK), zspec(Z_AV_BLK), tspec, tspec, tspec],
        out_specs=(ospec, ospec, ospec, ospec),
        compiler_params=_cparams(("parallel",)),
        name="rope",
    )(z, z, z, *tables)


def _log_sigmoid(x):
    return -(jnp.maximum(-x, 0.0) + jnp.log1p(jnp.exp(-jnp.abs(x))))


def _sigmoid(x):
    return 1.0 / (1.0 + jnp.exp(-x))


def _dot_hilo(a_bf, x):
    hi = x.astype(BF16)
    lo = (x - hi.astype(F32)).astype(BF16)
    return (jnp.dot(a_bf, hi, preferred_element_type=F32) + jnp.dot(a_bf, lo, preferred_element_type=F32))


def _dot_hilo_r(x, a_bf):
    hi = x.astype(BF16)
    lo = (x - hi.astype(F32)).astype(BF16)
    return (jnp.dot(hi, a_bf, preferred_element_type=F32) + jnp.dot(lo, a_bf, preferred_element_type=F32))


def _mlstm_kernel(n_valid, u_ref, mv_ref, mo_ref, gc_ref, gr_ref, cinit_ref, c0_ref, n0_ref, m0_ref,
                  cw_ref, cb_ref, bc_ref, br_ref, ng_ref,
                  y_ref, co_ref, no_ref, mo_out_ref,
                  ubuf, c_s, n_s, m_s):
    L = u_ref.shape[0]
    c_id = pl.program_id(1)

    @pl.when(c_id == 0)
    def _():
        ubuf[0:SUBLANES, :] = cinit_ref[...]
        c_s[...] = c0_ref[...]
        n_s[...] = n0_ref[...]
        m_s[...] = m0_ref[...]

    ubuf[SUBLANES:SUBLANES + L, :] = u_ref[...]
    y = ubuf[SUBLANES - 3:SUBLANES - 3 + L, :] * cw_ref[0:1, :] + cb_ref[...]
    for j in range(1, CONV_W):
        y = y + ubuf[SUBLANES - 3 + j:SUBLANES - 3 + j + L, :] * cw_ref[j:j + 1, :]
    qk = y * _sigmoid(y)
    ubuf[0:SUBLANES, :] = ubuf[L:L + SUBLANES, :]

    gcol = gc_ref[...] + bc_ref[...]
    grow = gr_ref[...] + br_ref[...]
    li_c, lf_c = gcol, _log_sigmoid(gcol)
    li_r, lf_r = grow, _log_sigmoid(grow)
    if n_valid < L:
        vc = lax.broadcasted_iota(jnp.int32, gcol.shape, 0) < n_valid
        vr = lax.broadcasted_iota(jnp.int32, grow.shape, 1) < n_valid
        li_c, lf_c = jnp.where(vc, li_c, NEG_BIG), jnp.where(vc, lf_c, 0.0)
        li_r, lf_r = jnp.where(vr, li_r, NEG_BIG), jnp.where(vr, lf_r, 0.0)
    t_i = lax.broadcasted_iota(jnp.int32, (L, L), 0)
    s_i = lax.broadcasted_iota(jnp.int32, (L, L), 1)
    tril = s_i <= t_i
    tri_lo = jnp.where(tril, 1.0, 0.0).astype(BF16)
    tri_up = jnp.where(t_i <= s_i, 1.0, 0.0).astype(BF16)
    b_c = _dot_hilo(tri_lo, lf_c)
    b_r = _dot_hilo_r(lf_r, tri_up)

    for h in range(M_HEADS):
        hs = slice(h * M_DK, (h + 1) * M_DK)
        q = qk[:, hs]
        k = qk[:, M_HEADS * M_DK + h * M_DK:M_HEADS * M_DK + (h + 1) * M_DK] * (M_DK ** -0.5)
        v = mv_ref[:, hs]
        qb, kb = q.astype(BF16), k.astype(BF16)
        a_row = li_r[h:h + 1, :] - b_r[M_HEADS + h:M_HEADS + h + 1, :]
        b_col = b_c[:, M_HEADS + h:M_HEADS + h + 1]
        a_col = li_c[:, h:h + 1] - b_col
        m_prev = m_s[h]
        c_prev = c_s[h]
        n_prev = n_s[h]
        a_mat = jnp.where(tril, jnp.broadcast_to(a_row, (L, L)), -jnp.inf)
        g_col = jnp.maximum(m_prev, jnp.max(a_mat, axis=1, keepdims=True))
        d_mat = jnp.exp(a_mat - g_col)
        w_inter = jnp.exp(m_prev - g_col)
        s_mat = lax.dot_general(qb, kb, (((1,), (1,)), ((), ())), preferred_element_type=F32) * d_mat
        cq = lax.dot_general(qb, c_prev.astype(BF16), (((1,), (1,)), ((), ())), preferred_element_type=F32)
        num = jnp.dot(s_mat.astype(BF16), v.astype(BF16), preferred_element_type=F32) + w_inter * cq
        den = jnp.sum(s_mat, axis=1, keepdims=True) + w_inter * jnp.sum(q * n_prev, axis=1, keepdims=True)
        m_t = b_col + g_col
        hh = num / jnp.maximum(jnp.abs(den), jnp.exp(-m_t))
        g_last = g_col[L - 1:L, :]
        w_col = jnp.exp(a_col - g_last)
        decay = jnp.exp(m_prev - g_last)
        vw = (v * w_col).astype(BF16)
        c_s[h] = decay * c_prev + lax.dot_general(vw, kb, (((0,), (0,)), ((), ())), preferred_element_type=F32)
        n_s[h] = decay * n_prev + jnp.sum(k * w_col, axis=0, keepdims=True)
        m_s[h] = b_col[L - 1:L, :] + g_last
        hn = hh * lax.rsqrt(jnp.mean(hh * hh, axis=-1, keepdims=True) + EPS)
        y_ref[:, hs] = (hn * ng_ref[:, hs] * _sigmoid(mo_ref[:, hs])).astype(y_ref.dtype)

    @pl.when(c_id == pl.num_programs(1) - 1)
    def _():
        co_ref[...] = c_s[...]
        no_ref[...] = n_s[...]
        mo_out_ref[...] = m_s[...]


def _mlstm(z, gates_t, conv_init, c0, n0, m0, prm, nb, nc, n_valid):
    L = CHUNK
    rows = nb * nc * L
    bsel = (lambda b: b) if c0.shape[0] == nb else (lambda b: 0)
    csel = (lambda b: b) if conv_init.shape[0] == nb else (lambda b: 0)
    row = lambda b, c: b * nc + c
    const = lambda shape: pl.BlockSpec(shape, lambda b, c: (0,) * len(shape))
    in_specs = [
        pl.BlockSpec((L, QK_M), lambda b, c: (row(b, c), 0)),
        pl.BlockSpec((L, M_WIDTH), lambda b, c: (row(b, c), Z_MV_BLK)),
        pl.BlockSpec((L, M_WIDTH), lambda b, c: (row(b, c), Z_MO_BLK)),
        pl.BlockSpec((L, LANES), lambda b, c: (row(b, c), Z_GATE_BLK)),
        pl.BlockSpec((SUBLANES, L), lambda b, c: (0, row(b, c))),
        pl.BlockSpec((None, SUBLANES, QK_M), lambda b, c: (csel(b), 0, 0)),
        pl.BlockSpec((None, M_HEADS, M_DV, M_DK), lambda b, c: (bsel(b), 0, 0, 0)),
        pl.BlockSpec((None, M_HEADS, 1, M_DK), lambda b, c: (bsel(b), 0, 0, 0)),
        pl.BlockSpec((None, M_HEADS, 1, 1), lambda b, c: (bsel(b), 0, 0, 0)),
        const((CONV_W, QK_M)), const((1, QK_M)), const((1, LANES)), const((SUBLANES, L)), const((1, M_WIDTH)),
    ]
    out_specs = (
        pl.BlockSpec((L, M_WIDTH), lambda b, c: (row(b, c), 0)),
        pl.BlockSpec((None, M_HEADS, M_DV, M_DK), lambda b, c: (b, 0, 0, 0)),
        pl.BlockSpec((None, M_HEADS, 1, M_DK), lambda b, c: (b, 0, 0, 0)),
        pl.BlockSpec((None, M_HEADS, 1, 1), lambda b, c: (b, 0, 0, 0)),
    )
    out_shape = (
        jax.ShapeDtypeStruct((rows, M_WIDTH), BF16),
        jax.ShapeDtypeStruct((nb, M_HEADS, M_DV, M_DK), F32),
        jax.ShapeDtypeStruct((nb, M_HEADS, 1, M_DK), F32),
        jax.ShapeDtypeStruct((nb, M_HEADS, 1, 1), F32),
    )
    return pl.pallas_call(
        functools.partial(_mlstm_kernel, n_valid),
        out_shape=out_shape,
        grid=(nb, nc),
        in_specs=in_specs,
        out_specs=out_specs,
        scratch_shapes=[pltpu.VMEM((L + SUBLANES, QK_M), F32), pltpu.VMEM((M_HEADS, M_DV, M_DK), F32),
                        pltpu.VMEM((M_HEADS, 1, M_DK), F32), pltpu.VMEM((M_HEADS, 1, 1), F32)],
        compiler_params=_cparams(("parallel", "arbitrary")),
        name="mlstm",
    )(z, z, z, z, gates_t, conv_init, c0, n0, m0,
      prm["conv_w"], prm["conv_b"], prm["b_if_row"], prm["b_if_col"], prm["mlstm_g"])


def _mstep_kernel(u_ref, mv_ref, mo_ref, g_ref, sc_ref, c0_ref, n0_ref, m0_ref,
                  cw_ref, cb_ref, bif_ref, ng_ref,
                  y_ref, co_ref, no_ref, mo_out_ref, conv_ref):
    u = u_ref[...]
    sc = sc_ref[...]
    y = sc[0:1, :] * cw_ref[0:1, :] + cb_ref[...]
    y = y + sc[1:2, :] * cw_ref[1:2, :]
    y = y + sc[2:3, :] * cw_ref[2:3, :]
    y = y + u * cw_ref[3:4, :]
    qk = y * _sigmoid(y)
    conv_ref[0:2, :] = sc[1:3, :]
    conv_ref[2:3, :] = u
    gates = g_ref[...] + bif_ref[...]
    lf_all = _log_sigmoid(gates)
    e_i = lax.broadcasted_iota(jnp.int32, (M_DV, M_DV), 0)
    e_j = lax.broadcasted_iota(jnp.int32, (M_DV, M_DV), 1)
    eye = e_i == e_j
    for h in range(M_HEADS):
        hs = slice(h * M_DK, (h + 1) * M_DK)
        q = qk[:, hs]
        k = qk[:, M_HEADS * M_DK + h * M_DK:M_HEADS * M_DK + (h + 1) * M_DK] * (M_DK ** -0.5)
        v = mv_ref[:, hs]
        li = gates[:, h:h + 1]
        lf = lf_all[:, M_HEADS + h:M_HEADS + h + 1]
        m_prev = m0_ref[h]
        c_prev = c0_ref[h]
        n_prev = n0_ref[h]
        m_t = jnp.maximum(lf + m_prev, li)
        d_w = jnp.exp(li - m_t)
        w_inter = jnp.exp(lf + m_prev - m_t)
        s = jnp.sum(q * k, axis=1, keepdims=True) * d_w
        q8 = jnp.broadcast_to(q, (SUBLANES, M_DK)).astype(BF16)
        cq = lax.dot_general(q8, c_prev.astype(BF16), (((1,), (1,)), ((), ())), preferred_element_type=F32)[0:1, :]
        num = s * v + w_inter * cq
        den = s + w_inter * jnp.sum(q * n_prev, axis=1, keepdims=True)
        hh = num / jnp.maximum(jnp.abs(den), jnp.exp(-m_t))
        vdiag = jnp.where(eye, jnp.broadcast_to(v * d_w, (M_DV, M_DV)), 0.0).astype(BF16)
        krows = jnp.broadcast_to(k, (M_DV, M_DK)).astype(BF16)
        co_ref[h] = w_inter * c_prev + jnp.dot(vdiag, krows, preferred_element_type=F32)
        no_ref[h] = w_inter * n_prev + d_w * k
        mo_out_ref[h] = m_t
        hn = hh * lax.rsqrt(jnp.mean(hh * hh, axis=-1, keepdims=True) + EPS)
        y_ref[:, hs] = (hn * ng_ref[:, hs] * _sigmoid(mo_ref[:, hs])).astype(y_ref.dtype)


def _mstep(z3, state_conv, c0, n0, m0, prm):
    nb = z3.shape[0]
    const = lambda shape: pl.BlockSpec(shape, lambda b: (0,) * len(shape))
    zspec = lambda w, blk: pl.BlockSpec((None, 1, w), lambda b, blk=blk: (b, 0, blk))
    st = lambda shape: pl.BlockSpec((None,) + shape, lambda b: (b,) + (0,) * len(shape))
    return pl.pallas_call(
        _mstep_kernel,
        out_shape=(jax.ShapeDtypeStruct((nb, 1, M_WIDTH), BF16),
                   jax.ShapeDtypeStruct((nb, M_HEADS, M_DV, M_DK), F32),
                   jax.ShapeDtypeStruct((nb, M_HEADS, 1, M_DK), F32),
                   jax.ShapeDtypeStruct((nb, M_HEADS, 1, 1), F32),
                   jax.ShapeDtypeStruct((nb, CONV_W - 1, QK_M), F32)),
        grid=(nb,),
        in_specs=[zspec(QK_M, 0), zspec(M_WIDTH, Z_MV_BLK), zspec(M_WIDTH, Z_MO_BLK), zspec(LANES, Z_GATE_BLK),
                  st((CONV_W - 1, QK_M)), st((M_HEADS, M_DV, M_DK)), st((M_HEADS, 1, M_DK)), st((M_HEADS, 1, 1)),
                  const((CONV_W, QK_M)), const((1, QK_M)), const((1, LANES)), const((1, M_WIDTH))],
        out_specs=(pl.BlockSpec((None, 1, M_WIDTH), lambda b: (b, 0, 0)),
                   st((M_HEADS, M_DV, M_DK)), st((M_HEADS, 1, M_DK)), st((M_HEADS, 1, 1)), st((CONV_W - 1, QK_M))),
        compiler_params=_cparams(("parallel",)),
        name="mstep",
    )(z3, z3, z3, z3, state_conv, c0, n0, m0,
      prm["conv_w"], prm["conv_b"], prm["b_if_row"], prm["mlstm_g"])


def _diff_lambda(l_ref):
    a = jnp.sum(l_ref[0:1, :] * l_ref[1:2, :], axis=1, keepdims=True)
    b = jnp.sum(l_ref[2:3, :] * l_ref[3:4, :], axis=1, keepdims=True)
    return jnp.exp(a) - jnp.exp(b) + LAM_INIT


def _class_allreduce(x, op):
    for sh in (A_HEADS, 2 * A_HEADS, 4 * A_HEADS, 8 * A_HEADS):
        x = op(x, pltpu.roll(x, sh, axis=1))
    return x


def _decode_update(kmat, vmat, n_lanes_valid, m_s, l_s, acc_s):
    nt = (((1,), (1,)), ((), ()))
    r_i = lax.broadcasted_iota(jnp.int32, (SUBLANES, LANES), 0)
    l_i = lax.broadcasted_iota(jnp.int32, (SUBLANES, LANES), 1)
    rsel = jnp.where((r_i < 2) & ((l_i >= A_DH) == (r_i == 1)), 1.0, 0.0).astype(BF16)
    diag = (l_i % A_HEADS) == r_i
    rows = kmat.shape[0]
    st = lax.dot_general(rsel, kmat.astype(BF16), nt, preferred_element_type=F32)
    tiles = [st[:, t * LANES:(t + 1) * LANES] for t in range(rows // LANES)]
    if n_lanes_valid is not None:
        tiles = [jnp.where(l_i + t * LANES < n_lanes_valid, x, -jnp.inf) for t, x in enumerate(tiles)]
    mb = tiles[0]
    for x in tiles[1:]:
        mb = jnp.maximum(mb, x)
    mb = _class_allreduce(mb, jnp.maximum)
    m_old = m_s[...]
    m_new = jnp.maximum(m_old, mb)
    alpha = jnp.exp2(m_old - m_new)
    ps = [jnp.exp2(x - m_new) for x in tiles]
    lsum = ps[0]
    for x in ps[1:]:
        lsum = lsum + x
    l_s[...] = alpha * l_s[...] + lsum
    m_s[...] = m_new
    pm = []
    for c in range(2):
        pm.append(jnp.concatenate(
            [jnp.where(diag, jnp.broadcast_to(x[c:c + 1, :], (SUBLANES, LANES)), 0.0) for x in ps], axis=1))
    pm = jnp.concatenate(pm, axis=0).astype(BF16)
    pv = jnp.dot(pm, vmat.astype(BF16), preferred_element_type=F32)
    acc_s[...] = _decode_colvec(alpha) * acc_s[...] + pv


def _decode_colvec(x_rep):
    r_i = lax.broadcasted_iota(jnp.int32, (SUBLANES, LANES), 0)
    l_i = lax.broadcasted_iota(jnp.int32, (SUBLANES, LANES), 1)
    cols = []
    for c in range(2):
        xb = jnp.broadcast_to(x_rep[c:c + 1, :], (SUBLANES, LANES))
        cols.append(jnp.sum(jnp.where(l_i == r_i, xb, 0.0), axis=1, keepdims=True))
    return jnp.concatenate(cols, axis=0)


def _attn_decode_kernel(n_meta, rb, g_pages, n_dec, dps, spp,
                        pt_ref, q_ref, k_ref, v_ref, km_ref, vm_ref, lam_ref, ag_ref, qd_ref, kn_ref, vn_ref,
                        ck_ref, cv_ref, o_ref, od_ref,
                        qq_s, mx_s, acc_s, s_s, kbuf, vbuf, sem, dm_s, dl_s, dacc_s):
    tq = q_ref.shape[0]
    nq = s_s.shape[0]
    b, h, i = pl.program_id(0), pl.program_id(1), pl.program_id(2)
    step = (b * pl.num_programs(1) + h) * pl.num_programs(2) + i

    def page_copies(d):
        slot, sb, pg = lax.rem(d, 2), lax.div(d, spp), lax.rem(d, spp)
        cps = []
        for t in range(g_pages):
            pidx = pt_ref[sb, pg * g_pages + t]
            cps.append(pltpu.make_async_copy(ck_ref.at[pidx], kbuf.at[slot, t], sem.at[0, slot]))
            cps.append(pltpu.make_async_copy(cv_ref.at[pidx], vbuf.at[slot, t], sem.at[1, slot]))
        return cps

    def issue(d):
        @pl.when(d < n_dec)
        def _():
            for cp in page_copies(d):
                cp.start()

    def decode_step(d):
        @pl.when(d < n_dec)
        def _():
            for cp in page_copies(d):
                cp.wait()
            slot, sb, pg = lax.rem(d, 2), lax.div(d, spp), lax.rem(d, spp)
            q = qd_ref[sb]

            @pl.when(pg == 0)
            def _():
                dm_s[...] = jnp.full(dm_s.shape, NEG_BIG, F32)
                dl_s[...] = jnp.zeros(dl_s.shape, F32)
                dacc_s[...] = jnp.zeros(dacc_s.shape, F32)
                pad = jnp.zeros((LANES - SUBLANES, LANES), F32)
                _decode_update(jnp.concatenate([kn_ref[sb] * q, pad], axis=0),
                               jnp.concatenate([vn_ref[sb], pad], axis=0), A_HEADS, dm_s, dl_s, dacc_s)

            kmat = jnp.concatenate([(kbuf[slot, t, :, 0] * q[None, :, :]).reshape(-1, LANES) for t in range(g_pages)],
                                   axis=0)
            vmat = jnp.concatenate([vbuf[slot, t, :, 0].reshape(-1, LANES) for t in range(g_pages)], axis=0)
            _decode_update(kmat, vmat, None, dm_s, dl_s, dacc_s)

            @pl.when(pg == spp - 1)
            def _():
                lam = _diff_lambda(lam_ref)
                l_rep = _class_allreduce(dl_s[...], jnp.add)
                o = dacc_s[...] / _decode_colvec(l_rep)
                o = o[0:A_HEADS, :] - lam * o[A_HEADS:2 * A_HEADS, :]
                on = o * lax.rsqrt(jnp.mean(o * o, axis=-1, keepdims=True) + EPS)
                od_ref[sb] = (on * ag_ref[...] * (1.0 - LAM_INIT)).astype(od_ref.dtype)

    d0 = step * dps

    @pl.when(step == 0)
    def _():
        issue(0)

    issue(d0 + 1)

    q = q_ref[...]
    lane = lax.broadcasted_iota(jnp.int32, q.shape, 1)
    zero = jnp.zeros_like(q)
    qq_s[0:tq, :] = jnp.where(lane < A_DH, q, zero)
    qq_s[tq:2 * tq, :] = jnp.where(lane >= A_DH, q, zero)
    nt = (((1,), (1,)), ((), ()))
    blocks = [(r0, r0 % tq) for r0 in range(0, 2 * tq, rb)]

    def scores(r0, kc):
        return lax.dot_general(qq_s[r0:r0 + rb, :], kc, nt, preferred_element_type=F32)

    def meta_scores(r0):
        s = scores(r0, km_ref[...])
        col = lax.broadcasted_iota(jnp.int32, s.shape, 1)
        return jnp.where(col < n_meta, s, -jnp.inf)

    def fold_max(m, s):
        for t in range(s.shape[1] // LANES):
            m = jnp.maximum(m, s[:, t * LANES:(t + 1) * LANES])
        return m

    def probs(s, m_rep):
        return jnp.exp2(s - jnp.concatenate([m_rep] * (s.shape[1] // LANES), axis=1)).astype(BF16)

    def with_ones(v):
        return jnp.concatenate([v, jnp.ones((v.shape[0], LANES), v.dtype)], axis=1)

    def pass1_chunk(j, width):
        kc = k_ref[pl.ds(pl.multiple_of(j * tq, tq), width), :]
        for r0, _ in blocks:
            s = scores(r0, kc)
            for w in range(width // tq):
                s_s[j + w, r0:r0 + rb, :] = s[:, w * tq:(w + 1) * tq]
            mx_s[r0:r0 + rb, :] = fold_max(mx_s[r0:r0 + rb, :], s)

    def pass2_chunk(j, width):
        v1 = with_ones(v_ref[pl.ds(pl.multiple_of(j * tq, tq), width), :])
        for r0, _ in blocks:
            s = jnp.concatenate([s_s[j + w, r0:r0 + rb, :] for w in range(width // tq)], axis=1)
            acc_s[r0:r0 + rb, :] += jnp.dot(probs(s, mx_s[r0:r0 + rb, :]), v1, preferred_element_type=F32)

    def full_chunks(chunk_fn):
        if nq >= 2:
            def pair(jj, carry):
                chunk_fn(2 * jj, 2 * tq)
                return carry

            lax.fori_loop(0, lax.shift_right_logical(i, 1), pair, 0)

            @pl.when(lax.rem(i, 2) == 1)
            def _():
                chunk_fn(i - 1, tq)

    dstart = pl.multiple_of(i * tq, tq)

    for r0, _ in blocks:
        mx_s[r0:r0 + rb, :] = meta_scores(r0)
    full_chunks(pass1_chunk)
    for r0, q0 in blocks:
        s = scores(r0, k_ref[pl.ds(dstart, q0 + rb), :])
        row = lax.broadcasted_iota(jnp.int32, s.shape, 0) + q0
        col = lax.broadcasted_iota(jnp.int32, s.shape, 1)
        s = jnp.where(col <= row, s, -jnp.inf)
        s_s[i, r0:r0 + rb, 0:q0 + rb] = s
        m = fold_max(mx_s[r0:r0 + rb, :], s)
        mx_s[r0:r0 + rb, :] = jnp.broadcast_to(jnp.max(m, axis=1, keepdims=True), (rb, LANES))

    vm1 = with_ones(vm_ref[...])
    for r0, _ in blocks:
        acc_s[r0:r0 + rb, :] = jnp.dot(probs(meta_scores(r0), mx_s[r0:r0 + rb, :]), vm1, preferred_element_type=F32)
    full_chunks(pass2_chunk)
    for r0, q0 in blocks:
        p = probs(s_s[i, r0:r0 + rb, 0:q0 + rb], mx_s[r0:r0 + rb, :])
        v1 = with_ones(v_ref[pl.ds(dstart, q0 + rb), :])
        acc_s[r0:r0 + rb, :] += jnp.dot(p, v1, preferred_element_type=F32)

    o0 = acc_s[0:tq, 0:A_DV] / acc_s[0:tq, A_DV:2 * A_DV]
    o1 = acc_s[tq:2 * tq, 0:A_DV] / acc_s[tq:2 * tq, A_DV:2 * A_DV]
    o = o0 - _diff_lambda(lam_ref) * o1
    on = o * lax.rsqrt(jnp.mean(o * o, axis=-1, keepdims=True) + EPS)
    o_ref[...] = (on * ag_ref[...] * (1.0 - LAM_INIT)).astype(o_ref.dtype)

    decode_step(d0)
    for u in range(1, dps):
        issue(d0 + u + 1)
        decode_step(d0 + u)


def _attn_decode(q, k, v, k_meta, v_meta, n_meta, qd, k_new, v_new, cache_k, cache_v, page_table, lam4, attn_g):
    bsz, t, _ = q.shape
    db, n_pages = page_table.shape
    page = cache_k.shape[1]
    tq = _pick(t, 512)
    rb = _pick(tq, 256)
    nq = t // tq
    g_pages = 8 if n_pages % 8 == 0 else (2 if n_pages % 2 == 0 else 1)
    spp = n_pages // g_pages
    n_dec = db * spp
    n_steps = bsz * A_HEADS * nq
    dps = -(-n_dec // n_steps)
    hspec = lambda rows: pl.BlockSpec((None, rows, LANES), lambda b, h, i, pt: (b, 0, h))
    const = lambda shape: pl.BlockSpec(shape, lambda b, h, i, pt: (0,) * len(shape))
    qblk = pl.BlockSpec((None, tq, LANES), lambda b, h, i, pt: (b, i, h))
    grid_spec = pltpu.PrefetchScalarGridSpec(
        num_scalar_prefetch=1,
        grid=(bsz, A_HEADS, nq),
        in_specs=[qblk, hspec(t), hspec(t),
                  pl.BlockSpec((LANES, LANES), lambda b, h, i, pt: (0, h)),
                  pl.BlockSpec((LANES, LANES), lambda b, h, i, pt: (0, h)),
                  const((4, A_DH)), const((1, A_DV)),
                  const((db, A_HEADS, A_DV)), const((db, A_HEADS, A_DV)), const((db, A_HEADS, A_DV)),
                  pl.BlockSpec(memory_space=pl.ANY), pl.BlockSpec(memory_space=pl.ANY)],
        out_specs=(qblk, const((db, A_HEADS, A_DV))),
        scratch_shapes=[pltpu.VMEM((2 * tq, LANES), BF16), pltpu.VMEM((2 * tq, LANES), F32),
                        pltpu.VMEM((2 * tq, 2 * A_DV), F32), pltpu.VMEM((nq, 2 * tq, tq), F32),
                        pltpu.VMEM((2, g_pages, page, 1, A_HEADS, A_DV), F32),
                        pltpu.VMEM((2, g_pages, page, 1, A_HEADS, A_DV), F32),
                        pltpu.SemaphoreType.DMA((2, 2)),
                        pltpu.VMEM((SUBLANES, LANES), F32), pltpu.VMEM((SUBLANES, LANES), F32),
                        pltpu.VMEM((2 * A_HEADS, A_DV), F32)],
    )
    return pl.pallas_call(
        functools.partial(_attn_decode_kernel, n_meta, rb, g_pages, n_dec, dps, spp),
        out_shape=(jax.ShapeDtypeStruct((bsz, t, A_WIDTH), BF16), jax.ShapeDtypeStruct((db, A_HEADS, A_DV), BF16)),
        grid_spec=grid_spec,
        compiler_params=_cparams(("arbitrary", "arbitrary", "arbitrary")),
        name="attn_decode",
    )(page_table, q, k, v, k_meta, v_meta, lam4, attn_g, qd, k_new, v_new, cache_k, cache_v)


def _merge_kernel(x_ref, ym_ref, ya_ref, ga_ref, gb_ref, wpa_ref, wpb_ref, wo_ref, o_ref):
    pa = jnp.dot(ym_ref[...], wpa_ref[...], preferred_element_type=F32)
    pb = jnp.dot(ya_ref[...], wpb_ref[...], preferred_element_type=F32)
    mixed = _sigmoid(ga_ref[...]) * pa + _sigmoid(gb_ref[...]) * pb
    o_ref[...] = x_ref[...] + jnp.dot(mixed.astype(BF16), wo_ref[...], preferred_element_type=F32)


def _merge(x, y_m, y_a, z, prm):
    m = x.shape[0]
    tm = _pick(m, 512)
    rspec = lambda w, blk=0: pl.BlockSpec((tm, w), lambda i, blk=blk: (i, blk))
    const = lambda shape: pl.BlockSpec(shape, lambda i: (0, 0))
    return pl.pallas_call(
        _merge_kernel,
        out_shape=jax.ShapeDtypeStruct((m, D_MODEL), F32),
        grid=(m // tm,),
        in_specs=[rspec(D_MODEL), rspec(M_WIDTH), rspec(A_WIDTH), rspec(D_MODEL, Z_GA_BLK), rspec(D_MODEL, Z_GB_BLK),
                  const((M_WIDTH, D_MODEL)), const((A_WIDTH, D_MODEL)), const((D_MODEL, D_MODEL))],
        out_specs=rspec(D_MODEL),
        compiler_params=_cparams(("parallel",)),
        name="merge",
    )(x, y_m, y_a, z, z, prm["w_pa"], prm["w_pb"], prm["w_out"])


def _mlp_kernel(x_ref, g1_ref, wu_ref, wd_ref, g2_ref, o_ref):
    x = x_ref[...]
    xn = (x * lax.rsqrt(jnp.mean(x * x, axis=-1, keepdims=True) + EPS) * g1_ref[...]).astype(BF16)
    acc = x
    fc = D_MODEL
    for f in range(D_FF // fc):
        hf = jnp.dot(xn, wu_ref[:, f * fc:(f + 1) * fc], preferred_element_type=F32)
        hf = jnp.square(jnp.maximum(hf, 0.0)).astype(BF16)
        acc = acc + jnp.dot(hf, wd_ref[f * fc:(f + 1) * fc, :], preferred_element_type=F32)
    o_ref[...] = acc * lax.rsqrt(jnp.mean(acc * acc, axis=-1, keepdims=True) + EPS) * g2_ref[...]


def _mlp(x, prm):
    m = x.shape[0]
    tm = _pick(m, 512)
    const = lambda shape: pl.BlockSpec(shape, lambda i: (0, 0))
    return pl.pallas_call(
        _mlp_kernel,
        out_shape=jax.ShapeDtypeStruct((m, D_MODEL), F32),
        grid=(m // tm,),
        in_specs=[pl.BlockSpec((tm, D_MODEL), lambda i: (i, 0)), const((1, D_MODEL)),
                  const((D_MODEL, D_FF)), const((D_FF, D_MODEL)), const((1, D_MODEL))],
        out_specs=pl.BlockSpec((tm, D_MODEL), lambda i: (i, 0)),
        compiler_params=_cparams(("parallel",)),
        name="mlp",
    )(x, prm["ffn_g"], prm["w_up"], prm["w_down"], prm["final_g"])


def kernel(x_prompt, x_sample, cache_k, cache_v, state_C, state_n, state_m, state_conv, page_table, meta_tokens, norm_mix_g, norm_ffn_g, w_in, b_if, conv_w, conv_b, mlstm_norm_g, lambda_q1, lambda_k1, lambda_q2, lambda_k2, attn_norm_g, w_proj_a, w_proj_b, w_out, w_up, w_down, final_norm_g):
    bsz, seq, d = x_prompt.shape
    dbsz, s_dec, _ = x_sample.shape
    n_pages = page_table.shape[1]
    past = n_pages * cache_k.shape[1]
    assert d == D_MODEL and s_dec == 1 and seq % CHUNK == 0 and norm_mix_g.shape[0] == 1
    assert N_META + dbsz <= CHUNK
    l = 0

    gate_lo = QK_M + 2 * M_WIDTH
    gate_hi = gate_lo + 2 * M_HEADS
    w = w_in[l]
    w_re = jnp.concatenate([w[:, :gate_lo], w[:, gate_hi:], w[:, gate_lo:gate_hi],
                            jnp.zeros((d, LANES - 2 * M_HEADS), w.dtype)], axis=1).astype(BF16)
    bif = b_if[l].astype(F32)
    prm = {
        "conv_w": conv_w[l].astype(F32), "conv_b": conv_b[l].astype(F32)[None, :],
        "b_if_row": jnp.pad(bif, (0, LANES - 2 * M_HEADS))[None, :],
        "b_if_col": jnp.broadcast_to(bif[:, None], (2 * M_HEADS, CHUNK)),
        "mlstm_g": mlstm_norm_g[l].astype(F32)[None, :],
        "w_pa": w_proj_a[l].astype(BF16), "w_pb": w_proj_b[l].astype(BF16), "w_out": w_out[l].astype(BF16),
        "ffn_g": norm_ffn_g[l].astype(F32)[None, :], "w_up": w_up[l].astype(BF16), "w_down": w_down[l].astype(BF16),
        "final_g": final_norm_g.astype(F32)[None, :],
    }
    mix_g = norm_mix_g[l].astype(F32)[None, :]
    lam4 = jnp.stack([lambda_q1[l], lambda_k1[l], lambda_q2[l], lambda_k2[l]]).astype(F32)
    attn_g = attn_norm_g[l].astype(F32)[None, :]

    xp = x_prompt.reshape(bsz * seq, d)
    xs = x_sample.reshape(dbsz, d)
    xe = jnp.concatenate([meta_tokens.astype(F32), xs, jnp.zeros((CHUNK - N_META - dbsz, d), F32)], axis=0)
    z_p = _proj(xp, mix_g, w_re)
    z_e = _proj(xe, mix_g, w_re)

    pos_e = jnp.concatenate([jnp.arange(N_META), jnp.full((dbsz,), past), jnp.zeros((CHUNK - N_META - dbsz,), jnp.int32)])
    q_p, k_p, kb_p, vb_p = _rope(z_p, _rope_tables(N_META + jnp.arange(seq)), seq)
    q_e, k_e, kb_e, vb_e = _rope(z_e, _rope_tables(pos_e), CHUNK)

    zero_state = (jnp.zeros((1, M_HEADS, M_DV, M_DK), F32), jnp.zeros((1, M_HEADS, 1, M_DK), F32),
                  jnp.zeros((1, M_HEADS, 1, 1), F32))
    gt_e = z_e[:, Z_GATE_COL:Z_GATE_COL + 2 * M_HEADS].T
    gt_p = z_p[:, Z_GATE_COL:Z_GATE_COL + 2 * M_HEADS].T
    _, c_m, n_m, m_m = _mlstm(z_e, gt_e, jnp.zeros((1, SUBLANES, QK_M), F32), *zero_state, prm, 1, 1, N_META)
    conv_init = z_e[N_META - SUBLANES:N_META, :QK_M][None]
    ym_p, c_p, n_p, m_p = _mlstm(z_p, gt_p, conv_init, c_m, n_m, m_m, prm, bsz, seq // CHUNK, CHUNK)
    z_s = z_e[N_META:N_META + dbsz]
    ym_s, c_s, n_s, m_s, conv_s = _mstep(
        z_s[:, None, :], state_conv[l].astype(F32), state_C[l].astype(F32),
        state_n[l].astype(F32)[:, :, None, :], state_m[l].astype(F32)[:, :, None, None], prm)

    hv = lambda a: a[N_META:N_META + dbsz].reshape(dbsz, A_HEADS, A_DV)
    v_s = z_s[:, Z_AV_BLK * A_QK:(Z_AV_BLK + 1) * A_QK]
    ya_p, ya_s = _attn_decode(
        q_p.reshape(bsz, seq, A_QK), kb_p.reshape(bsz, seq, A_QK), vb_p.reshape(bsz, seq, A_WIDTH), kb_e, vb_e, N_META,
        hv(q_e).astype(F32), hv(k_e), v_s.reshape(dbsz, A_HEADS, A_DV), cache_k, cache_v,
        page_table.astype(jnp.int32), lam4, attn_g)

    y_p = _mlp(_merge(xp, ym_p, ya_p.reshape(bsz * seq, A_WIDTH), z_p, prm), prm)
    y_s = _mlp(_merge(xs, ym_s.reshape(dbsz, M_WIDTH), ya_s.reshape(dbsz, A_WIDTH), z_s, prm), prm)

    def with_meta(meta_rows, main):
        mm = jnp.broadcast_to(meta_rows[None], (bsz, N_META, A_QK))
        return jnp.concatenate([mm, main.reshape(bsz, seq, A_QK)], axis=1).reshape(bsz, N_META + seq, 1, A_HEADS, A_DV)

    av = slice(Z_AV_BLK * A_QK, (Z_AV_BLK + 1) * A_QK)
    k_prompt = with_meta(k_e[:N_META], k_p)
    v_prompt = with_meta(z_e[:N_META, av], z_p[:, av])
    conv_prompt = z_p.reshape(bsz, seq, Z_COLS)[:, seq - (CONV_W - 1):, :QK_M][None]
    return (y_p.reshape(bsz, seq, d), y_s.reshape(dbsz, 1, d),
            k_prompt, v_prompt,
            c_p[None], n_p.reshape(1, bsz, M_HEADS, M_DK), m_p.reshape(1, bsz, M_HEADS), conv_prompt,
            k_e[N_META:N_META + dbsz].reshape(dbsz, 1, 1, A_HEADS, A_DV), v_s.reshape(dbsz, 1, 1, A_HEADS, A_DV),
            c_s[None], n_s.reshape(1, dbsz, M_HEADS, M_DK), m_s.reshape(1, dbsz, M_HEADS), conv_s[None])
```

```python
import functools
import math

import jax
import jax.numpy as jnp
import numpy as np
from jax import lax
from jax.experimental import pallas as pl
from jax.experimental.pallas import tpu as pltpu

F32 = jnp.float32
BF16 = jnp.bfloat16

D_MODEL = 1024
N_META = 16
M_HEADS = 4
M_DK = 128
M_DV = 128
M_WIDTH = M_HEADS * M_DV
QK_M = 2 * M_HEADS * M_DK
CONV_W = 4
NEG_BIG = -1e30
A_HEADS = 8
A_DH = 64
A_DV = 2 * A_DH
A_QK = A_HEADS * 2 * A_DH
A_WIDTH = A_HEADS * A_DV
ROT_DIM = A_DH // 4
ROPE_THETA = 500000.0
D_FF = 4 * D_MODEL
EPS = 1e-6
LAM_INIT = 0.8 - 0.6 * math.exp(-0.3 * 0)
Q_SCALE = (A_DH ** -0.5) * math.log2(math.e)

LANES = 128
SUBLANES = 8
VMEM_LIMIT_BYTES = 56 * 1024 * 1024

Z_QK = 0
Z_MV_BLK = 2
Z_MO_BLK = 3
Z_AQ_BLK = 2
Z_AK_BLK = 3
Z_AV_BLK = 4
Z_GA_BLK = 5
Z_GB_BLK = 6
Z_GATE_COL = 7 * 1024
Z_GATE_BLK = Z_GATE_COL // LANES
Z_COLS = Z_GATE_COL + LANES

CHUNK = 128


def _cparams(sem):
    return pltpu.CompilerParams(dimension_semantics=sem, vmem_limit_bytes=VMEM_LIMIT_BYTES)


def _pick(n, target):
    if n <= target:
        return n
    t = target
    while t >= SUBLANES:
        if n % t == 0 and t % SUBLANES == 0:
            return t
        t -= SUBLANES
    return n


def _proj_kernel(tn, x_ref, g_ref, w_ref, o_ref):
    x = x_ref[...]
    ms = jnp.mean(x * x, axis=-1, keepdims=True)
    xn = (x * lax.rsqrt(ms + EPS) * g_ref[...]).astype(BF16)
    for c0 in range(0, w_ref.shape[1], tn):
        o_ref[:, c0:c0 + tn] = jnp.dot(xn, w_ref[:, c0:c0 + tn], preferred_element_type=F32)


def _proj(x, g, w):
    m, d = x.shape
    n = w.shape[1]
    tm = _pick(m, 256)
    tn = n // 3 if (n % (3 * LANES) == 0 and n // 3 <= 4096) else _pick(n, 2048)
    return pl.pallas_call(
        functools.partial(_proj_kernel, tn),
        out_shape=jax.ShapeDtypeStruct((m, n), F32),
        grid=(m // tm,),
        in_specs=[pl.BlockSpec((tm, d), lambda i: (i, 0)),
                  pl.BlockSpec((1, d), lambda i: (0, 0)),
                  pl.BlockSpec((d, n), lambda i: (0, 0))],
        out_specs=pl.BlockSpec((tm, n), lambda i: (i, 0)),
        compiler_params=_cparams(("parallel",)),
        name="proj",
    )(x, g, w)


def _rope_kernel(cache_layout, q_ref, k_ref, v_ref, c_ref, sa_ref, sb_ref, *refs):
    if cache_layout:
        refs = refs[2:]
    qo_ref, kb_ref, vb_ref, ko_ref, vo_ref, *ks = refs
    tm = q_ref.shape[0]
    half = ROT_DIM // 2

    def rot(x):
        up = pltpu.roll(x, half, axis=1)
        dn = pltpu.roll(x, LANES - half, axis=1)
        return x * c_ref[...] + dn * sb_ref[...] + up * sa_ref[...]

    k_dst = ks[0] if cache_layout else ko_ref
    for j in range(A_HEADS):
        cs = slice(j * LANES, (j + 1) * LANES)
        qo_ref[:, cs] = (rot(q_ref[:, cs]) * Q_SCALE).astype(BF16)
        k = rot(k_ref[:, cs])
        kb_ref[:, cs] = k.astype(BF16)
        k_dst[:, cs] = k
    vb_ref[...] = v_ref[...].astype(BF16)
    if cache_layout:
        ko_ref[0] = k_dst[...].reshape(tm, A_HEADS, A_DV)
        vo_ref[0] = v_ref[...].reshape(tm, A_HEADS, A_DV)
    else:
        vo_ref[...] = v_ref[...]


def _rope_tables(pos):
    half = ROT_DIM // 2
    f32 = np.float32
    inv = (f32(ROPE_THETA) ** (-np.arange(0, ROT_DIM, 2, dtype=f32) / f32(ROT_DIM))).astype(f32)
    ang = pos.astype(f32)[:, None] * inv[None, :]
    cos, sin = np.cos(ang).astype(f32), np.sin(ang).astype(f32)
    p = pos.shape[0]
    one = np.ones((p, A_DH - ROT_DIM), f32)
    zero = np.zeros((p, A_DH - ROT_DIM), f32)
    zh = np.zeros((p, half), f32)
    c = np.concatenate([cos, cos, one], axis=1)
    sa = np.concatenate([zh, sin, zero], axis=1)
    sb = np.concatenate([-sin, zh, zero], axis=1)
    return tuple(jnp.asarray(np.concatenate([a, a], axis=1)) for a in (c, sa, sb))


def _rope(z, tables, rows_per_table, lead=None):
    m = z.shape[0]
    nb = m // rows_per_table
    n_lead = None if lead is None else lead[0].shape[0]
    tm = _pick(rows_per_table, 512)
    nt = rows_per_table // tm
    zspec = lambda blk: pl.BlockSpec((tm, A_QK), lambda i, blk=blk: (i, blk))
    tspec = pl.BlockSpec((tm, LANES), lambda i: (i % nt, 0))
    ospec = pl.BlockSpec((tm, A_QK), lambda i: (i, 0))
    if n_lead is None:
        fspec, fshape = ospec, jax.ShapeDtypeStruct((m, A_QK), F32)
    else:
        fspec = pl.BlockSpec((pl.Element(1), pl.Element(tm), pl.Element(A_HEADS), pl.Element(A_DV)),
                             lambda i: (i // nt, n_lead + (i % nt) * tm, 0, 0))
        fshape = jax.ShapeDtypeStruct((nb, n_lead + rows_per_table, A_HEADS, A_DV), F32)
    args, in_specs, aliases = [z, z, z, *tables], [zspec(Z_AQ_BLK), zspec(Z_AK_BLK), zspec(Z_AV_BLK), tspec, tspec, tspec], {}
    if lead is not None:
        for a in lead:
            full = jnp.broadcast_to(a[None], (nb,) + a.shape)
            args.append(jnp.pad(full, ((0, 0), (0, rows_per_table), (0, 0), (0, 0))))
            in_specs.append(pl.BlockSpec(memory_space=pl.ANY))
        aliases = {6: 3, 7: 4}
    return pl.pallas_call(
        functools.partial(_rope_kernel, lead is not None),
        out_shape=(jax.ShapeDtypeStruct((m, A_QK), BF16), jax.ShapeDtypeStruct((m, A_QK), BF16),
                   jax.ShapeDtypeStruct((m, A_WIDTH), BF16), fshape, fshape),
        grid=(m // tm,),
        in_specs=in_specs,
        out_specs=(ospec, ospec, ospec, fspec, fspec),
        scratch_shapes=[] if lead is None else [pltpu.VMEM((tm, A_QK), F32)],
        input_output_aliases=aliases,
        compiler_params=_cparams(("parallel",)),
        name="rope",
    )(*args)


def _log_sigmoid(x):
    return -(jnp.maximum(-x, 0.0) + jnp.log1p(jnp.exp(-jnp.abs(x))))


def _sigmoid(x):
    return 1.0 / (1.0 + jnp.exp(-x))


def _dot_hilo(a_bf, x):
    hi = x.astype(BF16)
    lo = (x - hi.astype(F32)).astype(BF16)
    return (jnp.dot(a_bf, hi, preferred_element_type=F32) + jnp.dot(a_bf, lo, preferred_element_type=F32))


def _dot_hilo_r(x, a_bf):
    hi = x.astype(BF16)
    lo = (x - hi.astype(F32)).astype(BF16)
    return (jnp.dot(hi, a_bf, preferred_element_type=F32) + jnp.dot(lo, a_bf, preferred_element_type=F32))


def _mlstm_kernel(n_valid, u_ref, mv_ref, mo_ref, gc_ref, cinit_ref, c0_ref, n0_ref, m0_ref,
                  cw_ref, cb_ref, bc_ref, br_ref, ng_ref,
                  y_ref, co_ref, no_ref, mo_out_ref,
                  ubuf, c_s, n_s, m_s):
    L = u_ref.shape[0]
    c_id = pl.program_id(1)

    @pl.when(c_id == 0)
    def _():
        ubuf[0:SUBLANES, :] = cinit_ref[...]
        c_s[...] = c0_ref[...]
        n_s[...] = n0_ref[...]
        m_s[...] = m0_ref[...]

    ubuf[SUBLANES:SUBLANES + L, :] = u_ref[...]
    y = ubuf[SUBLANES - 3:SUBLANES - 3 + L, :] * cw_ref[0:1, :] + cb_ref[...]
    for j in range(1, CONV_W):
        y = y + ubuf[SUBLANES - 3 + j:SUBLANES - 3 + j + L, :] * cw_ref[j:j + 1, :]
    qk = y * _sigmoid(y)
    ubuf[0:SUBLANES, :] = ubuf[L:L + SUBLANES, :]

    gpre = gc_ref[...]
    gcol = gpre + bc_ref[...]
    grow = gpre.T[0:2 * M_HEADS, :] + br_ref[...]
    li_c, lf_c = gcol, _log_sigmoid(gcol)
    li_r, lf_r = grow, _log_sigmoid(grow)
    if n_valid < L:
        vc = lax.broadcasted_iota(jnp.int32, gcol.shape, 0) < n_valid
        vr = lax.broadcasted_iota(jnp.int32, grow.shape, 1) < n_valid
        li_c, lf_c = jnp.where(vc, li_c, NEG_BIG), jnp.where(vc, lf_c, 0.0)
        li_r, lf_r = jnp.where(vr, li_r, NEG_BIG), jnp.where(vr, lf_r, 0.0)
    t_i = lax.broadcasted_iota(jnp.int32, (L, L), 0)
    s_i = lax.broadcasted_iota(jnp.int32, (L, L), 1)
    tril = s_i <= t_i
    tri_lo = jnp.where(tril, 1.0, 0.0).astype(BF16)
    tri_up = jnp.where(t_i <= s_i, 1.0, 0.0).astype(BF16)
    b_c = _dot_hilo(tri_lo, lf_c)
    b_r = _dot_hilo_r(lf_r, tri_up)

    for h in range(M_HEADS):
        hs = slice(h * M_DK, (h + 1) * M_DK)
        q = qk[:, hs]
        k = qk[:, M_HEADS * M_DK + h * M_DK:M_HEADS * M_DK + (h + 1) * M_DK] * (M_DK ** -0.5)
        v = mv_ref[:, hs]
        qb, kb = q.astype(BF16), k.astype(BF16)
        a_row = li_r[h:h + 1, :] - b_r[M_HEADS + h:M_HEADS + h + 1, :]
        b_col = b_c[:, M_HEADS + h:M_HEADS + h + 1]
        a_col = li_c[:, h:h + 1] - b_col
        m_prev = m_s[h]
        c_prev = c_s[h]
        n_prev = n_s[h]
        a_mat = jnp.where(tril, jnp.broadcast_to(a_row, (L, L)), -jnp.inf)
        g_col = jnp.maximum(m_prev, jnp.max(a_mat, axis=1, keepdims=True))
        d_mat = jnp.exp(a_mat - g_col)
        w_inter = jnp.exp(m_prev - g_col)
        s_mat = lax.dot_general(qb, kb, (((1,), (1,)), ((), ())), preferred_element_type=F32) * d_mat
        cq = lax.dot_general(qb, c_prev.astype(BF16), (((1,), (1,)), ((), ())), preferred_element_type=F32)
        num = jnp.dot(s_mat.astype(BF16), v.astype(BF16), preferred_element_type=F32) + w_inter * cq
        den = jnp.sum(s_mat, axis=1, keepdims=True) + w_inter * jnp.sum(q * n_prev, axis=1, keepdims=True)
        m_t = b_col + g_col
        hh = num / jnp.maximum(jnp.abs(den), jnp.exp(-m_t))
        g_last = g_col[L - 1:L, :]
        w_col = jnp.exp(a_col - g_last)
        decay = jnp.exp(m_prev - g_last)
        vw = (v * w_col).astype(BF16)
        c_s[h] = decay * c_prev + lax.dot_general(vw, kb, (((0,), (0,)), ((), ())), preferred_element_type=F32)
        n_s[h] = decay * n_prev + jnp.sum(k * w_col, axis=0, keepdims=True)
        m_s[h] = b_col[L - 1:L, :] + g_last
        hn = hh * lax.rsqrt(jnp.mean(hh * hh, axis=-1, keepdims=True) + EPS)
        y_ref[:, hs] = (hn * ng_ref[:, hs] * _sigmoid(mo_ref[:, hs])).astype(y_ref.dtype)

    @pl.when(c_id == pl.num_programs(1) - 1)
    def _():
        co_ref[...] = c_s[...]
        no_ref[...] = n_s[...]
        mo_out_ref[...] = m_s[...]


def _mlstm(z, conv_init, c0, n0, m0, prm, nb, nc, n_valid):
    L = CHUNK
    rows = nb * nc * L
    bsel = (lambda b: b) if c0.shape[0] == nb else (lambda b: 0)
    csel = (lambda b: b) if conv_init.shape[0] == nb else (lambda b: 0)
    row = lambda b, c: b * nc + c
    const = lambda shape: pl.BlockSpec(shape, lambda b, c: (0,) * len(shape))
    in_specs = [
        pl.BlockSpec((L, QK_M), lambda b, c: (row(b, c), 0)),
        pl.BlockSpec((L, M_WIDTH), lambda b, c: (row(b, c), Z_MV_BLK)),
        pl.BlockSpec((L, M_WIDTH), lambda b, c: (row(b, c), Z_MO_BLK)),
        pl.BlockSpec((L, LANES), lambda b, c: (row(b, c), Z_GATE_BLK)),
        pl.BlockSpec((None, SUBLANES, QK_M), lambda b, c: (csel(b), 0, 0)),
        pl.BlockSpec((None, M_HEADS, M_DV, M_DK), lambda b, c: (bsel(b), 0, 0, 0)),
        pl.BlockSpec((None, M_HEADS, 1, M_DK), lambda b, c: (bsel(b), 0, 0, 0)),
        pl.BlockSpec((None, M_HEADS, 1, 1), lambda b, c: (bsel(b), 0, 0, 0)),
        const((CONV_W, QK_M)), const((1, QK_M)), const((1, LANES)), const((SUBLANES, L)), const((1, M_WIDTH)),
    ]
    out_specs = (
        pl.BlockSpec((L, M_WIDTH), lambda b, c: (row(b, c), 0)),
        pl.BlockSpec((None, M_HEADS, M_DV, M_DK), lambda b, c: (b, 0, 0, 0)),
        pl.BlockSpec((None, M_HEADS, 1, M_DK), lambda b, c: (b, 0, 0, 0)),
        pl.BlockSpec((None, M_HEADS, 1, 1), lambda b, c: (b, 0, 0, 0)),
    )
    out_shape = (
        jax.ShapeDtypeStruct((rows, M_WIDTH), BF16),
        jax.ShapeDtypeStruct((nb, M_HEADS, M_DV, M_DK), F32),
        jax.ShapeDtypeStruct((nb, M_HEADS, 1, M_DK), F32),
        jax.ShapeDtypeStruct((nb, M_HEADS, 1, 1), F32),
    )
    return pl.pallas_call(
        functools.partial(_mlstm_kernel, n_valid),
        out_shape=out_shape,
        grid=(nb, nc),
        in_specs=in_specs,
        out_specs=out_specs,
        scratch_shapes=[pltpu.VMEM((L + SUBLANES, QK_M), F32), pltpu.VMEM((M_HEADS, M_DV, M_DK), F32),
                        pltpu.VMEM((M_HEADS, 1, M_DK), F32), pltpu.VMEM((M_HEADS, 1, 1), F32)],
        compiler_params=_cparams(("parallel", "arbitrary")),
        name="mlstm",
    )(z, z, z, z, conv_init, c0, n0, m0,
      prm["conv_w"], prm["conv_b"], prm["b_if_row"], prm["b_if_col"], prm["mlstm_g"])


def _mstep_kernel(u_ref, mv_ref, mo_ref, g_ref, sc_ref, c0_ref, n0_ref, m0_ref,
                  cw_ref, cb_ref, bif_ref, ng_ref,
                  y_ref, co_ref, no_ref, mo_out_ref, conv_ref):
    u = u_ref[...]
    sc = sc_ref[...]
    y = sc[0:1, :] * cw_ref[0:1, :] + cb_ref[...]
    y = y + sc[1:2, :] * cw_ref[1:2, :]
    y = y + sc[2:3, :] * cw_ref[2:3, :]
    y = y + u * cw_ref[3:4, :]
    qk = y * _sigmoid(y)
    conv_ref[0:2, :] = sc[1:3, :]
    conv_ref[2:3, :] = u
    gates = g_ref[...] + bif_ref[...]
    lf_all = _log_sigmoid(gates)
    e_i = lax.broadcasted_iota(jnp.int32, (M_DV, M_DV), 0)
    e_j = lax.broadcasted_iota(jnp.int32, (M_DV, M_DV), 1)
    eye = e_i == e_j
    for h in range(M_HEADS):
        hs = slice(h * M_DK, (h + 1) * M_DK)
        q = qk[:, hs]
        k = qk[:, M_HEADS * M_DK + h * M_DK:M_HEADS * M_DK + (h + 1) * M_DK] * (M_DK ** -0.5)
        v = mv_ref[:, hs]
        li = gates[:, h:h + 1]
        lf = lf_all[:, M_HEADS + h:M_HEADS + h + 1]
        m_prev = m0_ref[h]
        c_prev = c0_ref[h]
        n_prev = n0_ref[h]
        m_t = jnp.maximum(lf + m_prev, li)
        d_w = jnp.exp(li - m_t)
        w_inter = jnp.exp(lf + m_prev - m_t)
        s = jnp.sum(q * k, axis=1, keepdims=True) * d_w
        q8 = jnp.broadcast_to(q, (SUBLANES, M_DK)).astype(BF16)
        cq = lax.dot_general(q8, c_prev.astype(BF16), (((1,), (1,)), ((), ())), preferred_element_type=F32)[0:1, :]
        num = s * v + w_inter * cq
        den = s + w_inter * jnp.sum(q * n_prev, axis=1, keepdims=True)
        hh = num / jnp.maximum(jnp.abs(den), jnp.exp(-m_t))
        vdiag = jnp.where(eye, jnp.broadcast_to(v * d_w, (M_DV, M_DV)), 0.0).astype(BF16)
        krows = jnp.broadcast_to(k, (M_DV, M_DK)).astype(BF16)
        co_ref[h] = w_inter * c_prev + jnp.dot(vdiag, krows, preferred_element_type=F32)
        no_ref[h] = w_inter * n_prev + d_w * k
        mo_out_ref[h] = m_t
        hn = hh * lax.rsqrt(jnp.mean(hh * hh, axis=-1, keepdims=True) + EPS)
        y_ref[:, hs] = (hn * ng_ref[:, hs] * _sigmoid(mo_ref[:, hs])).astype(y_ref.dtype)


def _mstep(z3, state_conv, c0, n0, m0, prm):
    nb = z3.shape[0]
    const = lambda shape: pl.BlockSpec(shape, lambda b: (0,) * len(shape))
    zspec = lambda w, blk: pl.BlockSpec((None, 1, w), lambda b, blk=blk: (b, 0, blk))
    st = lambda shape: pl.BlockSpec((None,) + shape, lambda b: (b,) + (0,) * len(shape))
    return pl.pallas_call(
        _mstep_kernel,
        out_shape=(jax.ShapeDtypeStruct((nb, 1, M_WIDTH), BF16),
                   jax.ShapeDtypeStruct((nb, M_HEADS, M_DV, M_DK), F32),
                   jax.ShapeDtypeStruct((nb, M_HEADS, 1, M_DK), F32),
                   jax.ShapeDtypeStruct((nb, M_HEADS, 1, 1), F32),
                   jax.ShapeDtypeStruct((nb, CONV_W - 1, QK_M), F32)),
        grid=(nb,),
        in_specs=[zspec(QK_M, 0), zspec(M_WIDTH, Z_MV_BLK), zspec(M_WIDTH, Z_MO_BLK), zspec(LANES, Z_GATE_BLK),
                  st((CONV_W - 1, QK_M)), st((M_HEADS, M_DV, M_DK)), st((M_HEADS, 1, M_DK)), st((M_HEADS, 1, 1)),
                  const((CONV_W, QK_M)), const((1, QK_M)), const((1, LANES)), const((1, M_WIDTH))],
        out_specs=(pl.BlockSpec((None, 1, M_WIDTH), lambda b: (b, 0, 0)),
                   st((M_HEADS, M_DV, M_DK)), st((M_HEADS, 1, M_DK)), st((M_HEADS, 1, 1)), st((CONV_W - 1, QK_M))),
        compiler_params=_cparams(("parallel",)),
        name="mstep",
    )(z3, z3, z3, z3, state_conv, c0, n0, m0,
      prm["conv_w"], prm["conv_b"], prm["b_if_row"], prm["mlstm_g"])


def _diff_lambda(l_ref):
    a = jnp.sum(l_ref[0:1, :] * l_ref[1:2, :], axis=1, keepdims=True)
    b = jnp.sum(l_ref[2:3, :] * l_ref[3:4, :], axis=1, keepdims=True)
    return jnp.exp(a) - jnp.exp(b) + LAM_INIT


def _class_allreduce(x, op):
    for sh in (A_HEADS, 2 * A_HEADS, 4 * A_HEADS, 8 * A_HEADS):
        x = op(x, pltpu.roll(x, sh, axis=1))
    return x


def _decode_update(kmat, vmat, n_lanes_valid, m_s, l_s, acc_s):
    nt = (((1,), (1,)), ((), ()))
    r_i = lax.broadcasted_iota(jnp.int32, (SUBLANES, LANES), 0)
    l_i = lax.broadcasted_iota(jnp.int32, (SUBLANES, LANES), 1)
    rsel = jnp.where((r_i < 2) & ((l_i >= A_DH) == (r_i == 1)), 1.0, 0.0).astype(BF16)
    diag = (l_i % A_HEADS) == r_i
    rows = kmat.shape[0]
    st = lax.dot_general(rsel, kmat.astype(BF16), nt, preferred_element_type=F32)
    tiles = [st[:, t * LANES:(t + 1) * LANES] for t in range(rows // LANES)]
    if n_lanes_valid is not None:
        tiles = [jnp.where(l_i + t * LANES < n_lanes_valid, x, -jnp.inf) for t, x in enumerate(tiles)]
    mb = tiles[0]
    for x in tiles[1:]:
        mb = jnp.maximum(mb, x)
    mb = _class_allreduce(mb, jnp.maximum)
    m_old = m_s[...]
    m_new = jnp.maximum(m_old, mb)
    alpha = jnp.exp2(m_old - m_new)
    ps = [jnp.exp2(x - m_new) for x in tiles]
    lsum = ps[0]
    for x in ps[1:]:
        lsum = lsum + x
    l_s[...] = alpha * l_s[...] + lsum
    m_s[...] = m_new
    pm = []
    for c in range(2):
        pm.append(jnp.concatenate(
            [jnp.where(diag, jnp.broadcast_to(x[c:c + 1, :], (SUBLANES, LANES)), 0.0) for x in ps], axis=1))
    pm = jnp.concatenate(pm, axis=0).astype(BF16)
    pv = jnp.dot(pm, vmat.astype(BF16), preferred_element_type=F32)
    acc_s[...] = _decode_colvec(alpha) * acc_s[...] + pv


def _decode_colvec(x_rep):
    r_i = lax.broadcasted_iota(jnp.int32, (SUBLANES, LANES), 0)
    l_i = lax.broadcasted_iota(jnp.int32, (SUBLANES, LANES), 1)
    cols = []
    for c in range(2):
        xb = jnp.broadcast_to(x_rep[c:c + 1, :], (SUBLANES, LANES))
        cols.append(jnp.sum(jnp.where(l_i == r_i, xb, 0.0), axis=1, keepdims=True))
    return jnp.concatenate(cols, axis=0)


def _attn_decode_kernel(n_meta, rb, g_pages, n_dec, dps, spp,
                        pt_ref, q_ref, k_ref, v_ref, km_ref, vm_ref, lam_ref, ag_ref, qd_ref, kn_ref, vn_ref,
                        ck_ref, cv_ref, o_ref, od_ref,
                        qq_s, mx_s, acc_s, s_s, kbuf, vbuf, sem, dm_s, dl_s, dacc_s):
    tq = q_ref.shape[0]
    nq = s_s.shape[0]
    b, h, i = pl.program_id(0), pl.program_id(1), pl.program_id(2)
    step = (b * pl.num_programs(1) + h) * pl.num_programs(2) + i

    def page_copies(d):
        slot, sb, pg = lax.rem(d, 2), lax.div(d, spp), lax.rem(d, spp)
        cps = []
        for t in range(g_pages):
            pidx = pt_ref[sb, pg * g_pages + t]
            cps.append(pltpu.make_async_copy(ck_ref.at[pidx], kbuf.at[slot, t], sem.at[0, slot]))
            cps.append(pltpu.make_async_copy(cv_ref.at[pidx], vbuf.at[slot, t], sem.at[1, slot]))
        return cps

    def issue(d):
        @pl.when(d < n_dec)
        def _():
            for cp in page_copies(d):
                cp.start()

    def decode_step(d):
        @pl.when(d < n_dec)
        def _():
            for cp in page_copies(d):
                cp.wait()
            slot, sb, pg = lax.rem(d, 2), lax.div(d, spp), lax.rem(d, spp)
            q = qd_ref[sb]

            @pl.when(pg == 0)
            def _():
                dm_s[...] = jnp.full(dm_s.shape, NEG_BIG, F32)
                dl_s[...] = jnp.zeros(dl_s.shape, F32)
                dacc_s[...] = jnp.zeros(dacc_s.shape, F32)
                pad = jnp.zeros((LANES - SUBLANES, LANES), F32)
                _decode_update(jnp.concatenate([kn_ref[sb] * q, pad], axis=0),
                               jnp.concatenate([vn_ref[sb], pad], axis=0), A_HEADS, dm_s, dl_s, dacc_s)

            kmat = jnp.concatenate([(kbuf[slot, t, :, 0] * q[None, :, :]).reshape(-1, LANES) for t in range(g_pages)],
                                   axis=0)
            vmat = jnp.concatenate([vbuf[slot, t, :, 0].reshape(-1, LANES) for t in range(g_pages)], axis=0)
            _decode_update(kmat, vmat, None, dm_s, dl_s, dacc_s)

            @pl.when(pg == spp - 1)
            def _():
                lam = _diff_lambda(lam_ref)
                l_rep = _class_allreduce(dl_s[...], jnp.add)
                o = dacc_s[...] / _decode_colvec(l_rep)
                o = o[0:A_HEADS, :] - lam * o[A_HEADS:2 * A_HEADS, :]
                on = o * lax.rsqrt(jnp.mean(o * o, axis=-1, keepdims=True) + EPS)
                od_ref[sb] = (on * ag_ref[...] * (1.0 - LAM_INIT)).astype(od_ref.dtype)

    d0 = step * dps

    @pl.when(step == 0)
    def _():
        issue(0)

    issue(d0 + 1)

    q = q_ref[...]
    lane = lax.broadcasted_iota(jnp.int32, q.shape, 1)
    zero = jnp.zeros_like(q)
    qq_s[0:tq, :] = jnp.where(lane < A_DH, q, zero)
    qq_s[tq:2 * tq, :] = jnp.where(lane >= A_DH, q, zero)
    nt = (((1,), (1,)), ((), ()))
    blocks = [(r0, r0 % tq) for r0 in range(0, 2 * tq, rb)]

    def scores(r0, kc):
        return lax.dot_general(qq_s[r0:r0 + rb, :], kc, nt, preferred_element_type=F32)

    def meta_scores(r0):
        s = scores(r0, km_ref[...])
        col = lax.broadcasted_iota(jnp.int32, s.shape, 1)
        return jnp.where(col < n_meta, s, -jnp.inf)

    def fold_max(m, s):
        for t in range(s.shape[1] // LANES):
            m = jnp.maximum(m, s[:, t * LANES:(t + 1) * LANES])
        return m

    def probs(s, m_rep):
        return jnp.exp2(s - jnp.concatenate([m_rep] * (s.shape[1] // LANES), axis=1)).astype(BF16)

    def with_ones(v):
        return jnp.concatenate([v, jnp.ones((v.shape[0], LANES), v.dtype)], axis=1)

    def pass1_chunk(j, width):
        kc = k_ref[pl.ds(pl.multiple_of(j * tq, tq), width), :]
        for r0, _ in blocks:
            s = scores(r0, kc)
            for w in range(width // tq):
                s_s[j + w, r0:r0 + rb, :] = s[:, w * tq:(w + 1) * tq]
            mx_s[r0:r0 + rb, :] = fold_max(mx_s[r0:r0 + rb, :], s)

    def pass2_chunk(j, width):
        v1 = with_ones(v_ref[pl.ds(pl.multiple_of(j * tq, tq), width), :])
        for r0, _ in blocks:
            s = jnp.concatenate([s_s[j + w, r0:r0 + rb, :] for w in range(width // tq)], axis=1)
            acc_s[r0:r0 + rb, :] += jnp.dot(probs(s, mx_s[r0:r0 + rb, :]), v1, preferred_element_type=F32)

    def full_chunks(chunk_fn):
        if nq >= 2:
            def pair(jj, carry):
                chunk_fn(2 * jj, 2 * tq)
                return carry

            lax.fori_loop(0, lax.shift_right_logical(i, 1), pair, 0)

            @pl.when(lax.rem(i, 2) == 1)
            def _():
                chunk_fn(i - 1, tq)

    dstart = pl.multiple_of(i * tq, tq)

    for r0, _ in blocks:
        mx_s[r0:r0 + rb, :] = meta_scores(r0)
    full_chunks(pass1_chunk)
    for r0, q0 in blocks:
        s = scores(r0, k_ref[pl.ds(dstart, q0 + rb), :])
        row = lax.broadcasted_iota(jnp.int32, s.shape, 0) + q0
        col = lax.broadcasted_iota(jnp.int32, s.shape, 1)
        s = jnp.where(col <= row, s, -jnp.inf)
        s_s[i, r0:r0 + rb, 0:q0 + rb] = s
        m = fold_max(mx_s[r0:r0 + rb, :], s)
        mx_s[r0:r0 + rb, :] = jnp.broadcast_to(jnp.max(m, axis=1, keepdims=True), (rb, LANES))

    vm1 = with_ones(vm_ref[...])
    for r0, _ in blocks:
        acc_s[r0:r0 + rb, :] = jnp.dot(probs(meta_scores(r0), mx_s[r0:r0 + rb, :]), vm1, preferred_element_type=F32)
    full_chunks(pass2_chunk)
    for r0, q0 in blocks:
        p = probs(s_s[i, r0:r0 + rb, 0:q0 + rb], mx_s[r0:r0 + rb, :])
        v1 = with_ones(v_ref[pl.ds(dstart, q0 + rb), :])
        acc_s[r0:r0 + rb, :] += jnp.dot(p, v1, preferred_element_type=F32)

    o0 = acc_s[0:tq, 0:A_DV] / acc_s[0:tq, A_DV:2 * A_DV]
    o1 = acc_s[tq:2 * tq, 0:A_DV] / acc_s[tq:2 * tq, A_DV:2 * A_DV]
    o = o0 - _diff_lambda(lam_ref) * o1
    on = o * lax.rsqrt(jnp.mean(o * o, axis=-1, keepdims=True) + EPS)
    o_ref[...] = (on * ag_ref[...] * (1.0 - LAM_INIT)).astype(o_ref.dtype)

    decode_step(d0)
    for u in range(1, dps):
        issue(d0 + u + 1)
        decode_step(d0 + u)


def _attn_decode(q, k, v, k_meta, v_meta, n_meta, qd, k_new, v_new, cache_k, cache_v, page_table, lam4, attn_g):
    bsz, t, _ = q.shape
    db, n_pages = page_table.shape
    page = cache_k.shape[1]
    tq = _pick(t, 512)
    rb = _pick(tq, 256)
    nq = t // tq
    g_pages = 8 if n_pages % 8 == 0 else (2 if n_pages % 2 == 0 else 1)
    spp = n_pages // g_pages
    n_dec = db * spp
    n_steps = bsz * A_HEADS * nq
    dps = -(-n_dec // n_steps)
    hspec = lambda rows: pl.BlockSpec((None, rows, LANES), lambda b, h, i, pt: (b, 0, h))
    const = lambda shape: pl.BlockSpec(shape, lambda b, h, i, pt: (0,) * len(shape))
    qblk = pl.BlockSpec((None, tq, LANES), lambda b, h, i, pt: (b, i, h))
    grid_spec = pltpu.PrefetchScalarGridSpec(
        num_scalar_prefetch=1,
        grid=(bsz, A_HEADS, nq),
        in_specs=[qblk, hspec(t), hspec(t),
                  pl.BlockSpec((LANES, LANES), lambda b, h, i, pt: (0, h)),
                  pl.BlockSpec((LANES, LANES), lambda b, h, i, pt: (0, h)),
                  const((4, A_DH)), const((1, A_DV)),
                  const((db, A_HEADS, A_DV)), const((db, A_HEADS, A_DV)), const((db, A_HEADS, A_DV)),
                  pl.BlockSpec(memory_space=pl.ANY), pl.BlockSpec(memory_space=pl.ANY)],
        out_specs=(qblk, const((db, A_HEADS, A_DV))),
        scratch_shapes=[pltpu.VMEM((2 * tq, LANES), BF16), pltpu.VMEM((2 * tq, LANES), F32),
                        pltpu.VMEM((2 * tq, 2 * A_DV), F32), pltpu.VMEM((nq, 2 * tq, tq), F32),
                        pltpu.VMEM((2, g_pages, page, 1, A_HEADS, A_DV), F32),
                        pltpu.VMEM((2, g_pages, page, 1, A_HEADS, A_DV), F32),
                        pltpu.SemaphoreType.DMA((2, 2)),
                        pltpu.VMEM((SUBLANES, LANES), F32), pltpu.VMEM((SUBLANES, LANES), F32),
                        pltpu.VMEM((2 * A_HEADS, A_DV), F32)],
    )
    return pl.pallas_call(
        functools.partial(_attn_decode_kernel, n_meta, rb, g_pages, n_dec, dps, spp),
        out_shape=(jax.ShapeDtypeStruct((bsz, t, A_WIDTH), BF16), jax.ShapeDtypeStruct((db, A_HEADS, A_DV), BF16)),
        grid_spec=grid_spec,
        compiler_params=_cparams(("arbitrary", "arbitrary", "arbitrary")),
        name="attn_decode",
    )(page_table, q, k, v, k_meta, v_meta, lam4, attn_g, qd, k_new, v_new, cache_k, cache_v)


def _merge_kernel(x_ref, ym_ref, ya_ref, ga_ref, gb_ref, wpa_ref, wpb_ref, wo_ref, o_ref):
    pa = jnp.dot(ym_ref[...], wpa_ref[...], preferred_element_type=F32)
    pb = jnp.dot(ya_ref[...], wpb_ref[...], preferred_element_type=F32)
    mixed = _sigmoid(ga_ref[...]) * pa + _sigmoid(gb_ref[...]) * pb
    o_ref[...] = x_ref[...] + jnp.dot(mixed.astype(BF16), wo_ref[...], preferred_element_type=F32)


def _merge(x, y_m, y_a, z, prm):
    m = x.shape[0]
    tm = _pick(m, 512)
    rspec = lambda w, blk=0: pl.BlockSpec((tm, w), lambda i, blk=blk: (i, blk))
    const = lambda shape: pl.BlockSpec(shape, lambda i: (0, 0))
    return pl.pallas_call(
        _merge_kernel,
        out_shape=jax.ShapeDtypeStruct((m, D_MODEL), F32),
        grid=(m // tm,),
        in_specs=[rspec(D_MODEL), rspec(M_WIDTH), rspec(A_WIDTH), rspec(D_MODEL, Z_GA_BLK), rspec(D_MODEL, Z_GB_BLK),
                  const((M_WIDTH, D_MODEL)), const((A_WIDTH, D_MODEL)), const((D_MODEL, D_MODEL))],
        out_specs=rspec(D_MODEL),
        compiler_params=_cparams(("parallel",)),
        name="merge",
    )(x, y_m, y_a, z, z, prm["w_pa"], prm["w_pb"], prm["w_out"])


def _mlp_kernel(x_ref, g1_ref, wu_ref, wd_ref, g2_ref, o_ref):
    x = x_ref[...]
    xn = (x * lax.rsqrt(jnp.mean(x * x, axis=-1, keepdims=True) + EPS) * g1_ref[...]).astype(BF16)
    acc = x
    fc = D_MODEL
    for f in range(D_FF // fc):
        hf = jnp.dot(xn, wu_ref[:, f * fc:(f + 1) * fc], preferred_element_type=F32)
        hf = jnp.square(jnp.maximum(hf, 0.0)).astype(BF16)
        acc = acc + jnp.dot(hf, wd_ref[f * fc:(f + 1) * fc, :], preferred_element_type=F32)
    o_ref[...] = acc * lax.rsqrt(jnp.mean(acc * acc, axis=-1, keepdims=True) + EPS) * g2_ref[...]


def _mlp(x, prm):
    m = x.shape[0]
    tm = _pick(m, 512)
    const = lambda shape: pl.BlockSpec(shape, lambda i: (0, 0))
    return pl.pallas_call(
        _mlp_kernel,
        out_shape=jax.ShapeDtypeStruct((m, D_MODEL), F32),
        grid=(m // tm,),
        in_specs=[pl.BlockSpec((tm, D_MODEL), lambda i: (i, 0)), const((1, D_MODEL)),
                  const((D_MODEL, D_FF)), const((D_FF, D_MODEL)), const((1, D_MODEL))],
        out_specs=pl.BlockSpec((tm, D_MODEL), lambda i: (i, 0)),
        compiler_params=_cparams(("parallel",)),
        name="mlp",
    )(x, prm["ffn_g"], prm["w_up"], prm["w_down"], prm["final_g"])


def kernel(x_prompt, x_sample, cache_k, cache_v, state_C, state_n, state_m, state_conv, page_table, meta_tokens, norm_mix_g, norm_ffn_g, w_in, b_if, conv_w, conv_b, mlstm_norm_g, lambda_q1, lambda_k1, lambda_q2, lambda_k2, attn_norm_g, w_proj_a, w_proj_b, w_out, w_up, w_down, final_norm_g):
    bsz, seq, d = x_prompt.shape
    dbsz, s_dec, _ = x_sample.shape
    n_pages = page_table.shape[1]
    past = n_pages * cache_k.shape[1]
    assert d == D_MODEL and s_dec == 1 and seq % CHUNK == 0 and norm_mix_g.shape[0] == 1
    assert N_META + dbsz <= CHUNK
    l = 0

    gate_lo = QK_M + 2 * M_WIDTH
    gate_hi = gate_lo + 2 * M_HEADS
    w = w_in[l]
    w_re = jnp.concatenate([w[:, :gate_lo], w[:, gate_hi:], w[:, gate_lo:gate_hi],
                            jnp.zeros((d, LANES - 2 * M_HEADS), w.dtype)], axis=1).astype(BF16)
    bif = b_if[l].astype(F32)
    prm = {
        "conv_w": conv_w[l].astype(F32), "conv_b": conv_b[l].astype(F32)[None, :],
        "b_if_row": jnp.pad(bif, (0, LANES - 2 * M_HEADS))[None, :],
        "b_if_col": jnp.broadcast_to(bif[:, None], (2 * M_HEADS, CHUNK)),
        "mlstm_g": mlstm_norm_g[l].astype(F32)[None, :],
        "w_pa": w_proj_a[l].astype(BF16), "w_pb": w_proj_b[l].astype(BF16), "w_out": w_out[l].astype(BF16),
        "ffn_g": norm_ffn_g[l].astype(F32)[None, :], "w_up": w_up[l].astype(BF16), "w_down": w_down[l].astype(BF16),
        "final_g": final_norm_g.astype(F32)[None, :],
    }
    mix_g = norm_mix_g[l].astype(F32)[None, :]
    lam4 = jnp.stack([lambda_q1[l], lambda_k1[l], lambda_q2[l], lambda_k2[l]]).astype(F32)
    attn_g = attn_norm_g[l].astype(F32)[None, :]

    xp = x_prompt.reshape(bsz * seq, d)
    xs = x_sample.reshape(dbsz, d)
    xe = jnp.concatenate([meta_tokens.astype(F32), xs, jnp.zeros((CHUNK - N_META - dbsz, d), F32)], axis=0)
    z_p = _proj(xp, mix_g, w_re)
    z_e = _proj(xe, mix_g, w_re)

    pos_e = np.concatenate([np.arange(N_META), np.full((dbsz,), past), np.zeros((CHUNK - N_META - dbsz,), np.int64)])
    q_e, kb_e, vb_e, k_e, v_e = _rope(z_e, _rope_tables(pos_e), CHUNK)
    heads = lambda a: a.reshape(a.shape[0], A_HEADS, A_DV)
    q_p, kb_p, vb_p, k_p, v_p = _rope(z_p, _rope_tables(N_META + np.arange(seq)), seq,
                                      (heads(k_e[:N_META]), heads(v_e[:N_META])))

    zero_state = (jnp.zeros((1, M_HEADS, M_DV, M_DK), F32), jnp.zeros((1, M_HEADS, 1, M_DK), F32),
                  jnp.zeros((1, M_HEADS, 1, 1), F32))
    _, c_m, n_m, m_m = _mlstm(z_e, jnp.zeros((1, SUBLANES, QK_M), F32), *zero_state, prm, 1, 1, N_META)
    conv_init = z_e[N_META - SUBLANES:N_META, :QK_M][None]
    ym_p, c_p, n_p, m_p = _mlstm(z_p, conv_init, c_m, n_m, m_m, prm, bsz, seq // CHUNK, CHUNK)
    z_s = z_e[N_META:N_META + dbsz]
    ym_s, c_s, n_s, m_s, conv_s = _mstep(
        z_s[:, None, :], state_conv[l].astype(F32), state_C[l].astype(F32),
        state_n[l].astype(F32)[:, :, None, :], state_m[l].astype(F32)[:, :, None, None], prm)

    hv = lambda a: a[N_META:N_META + dbsz].reshape(dbsz, A_HEADS, A_DV)
    v_s = v_e[N_META:N_META + dbsz]
    ya_p, ya_s = _attn_decode(
        q_p.reshape(bsz, seq, A_QK), kb_p.reshape(bsz, seq, A_QK), vb_p.reshape(bsz, seq, A_WIDTH), kb_e, vb_e, N_META,
        hv(q_e).astype(F32), hv(k_e), v_s.reshape(dbsz, A_HEADS, A_DV), cache_k, cache_v,
        page_table.astype(jnp.int32), lam4, attn_g)

    y_p = _mlp(_merge(xp, ym_p, ya_p.reshape(bsz * seq, A_WIDTH), z_p, prm), prm)
    y_s = _mlp(_merge(xs, ym_s.reshape(dbsz, M_WIDTH), ya_s.reshape(dbsz, A_WIDTH), z_s, prm), prm)

    k_prompt = k_p.reshape(bsz, N_META + seq, 1, A_HEADS, A_DV)
    v_prompt = v_p.reshape(bsz, N_META + seq, 1, A_HEADS, A_DV)
    conv_prompt = z_p.reshape(bsz, seq, Z_COLS)[:, seq - (CONV_W - 1):, :QK_M][None]
    return (y_p.reshape(bsz, seq, d), y_s.reshape(dbsz, 1, d),
            k_prompt, v_prompt,
            c_p[None], n_p.reshape(1, bsz, M_HEADS, M_DK), m_p.reshape(1, bsz, M_HEADS), conv_prompt,
            k_e[N_META:N_META + dbsz].reshape(dbsz, 1, 1, A_HEADS, A_DV), v_s.reshape(dbsz, 1, 1, A_HEADS, A_DV),
            c_s[None], n_s.reshape(1, dbsz, M_HEADS, M_DK), m_s.reshape(1, dbsz, M_HEADS), conv_s[None])
```

```python
import functools
import math

import jax
import jax.numpy as jnp
import numpy as np
from jax import lax
from jax.experimental import pallas as pl
from jax.experimental.pallas import tpu as pltpu

F32 = jnp.float32
BF16 = jnp.bfloat16

D_MODEL = 1024
N_META = 16
M_HEADS = 4
M_DK = 128
M_DV = 128
M_WIDTH = M_HEADS * M_DV
QK_M = 2 * M_HEADS * M_DK
CONV_W = 4
NEG_BIG = -1e30
A_HEADS = 8
A_DH = 64
A_DV = 2 * A_DH
A_QK = A_HEADS * 2 * A_DH
A_WIDTH = A_HEADS * A_DV
ROT_DIM = A_DH // 4
ROPE_THETA = 500000.0
D_FF = 4 * D_MODEL
EPS = 1e-6
LAM_INIT = 0.8 - 0.6 * math.exp(-0.3 * 0)
Q_SCALE = (A_DH ** -0.5) * math.log2(math.e)

LANES = 128
SUBLANES = 8
VMEM_LIMIT_BYTES = 56 * 1024 * 1024

Z_QK = 0
Z_MV_BLK = 2
Z_MO_BLK = 3
Z_AQ_BLK = 2
Z_AK_BLK = 3
Z_AV_BLK = 4
Z_GA_BLK = 5
Z_GB_BLK = 6
Z_GATE_COL = 7 * 1024
Z_GATE_BLK = Z_GATE_COL // LANES
Z_COLS = Z_GATE_COL + LANES

CHUNK = 128


def _cparams(sem):
    return pltpu.CompilerParams(dimension_semantics=sem, vmem_limit_bytes=VMEM_LIMIT_BYTES)


def _pick(n, target):
    if n <= target:
        return n
    t = target
    while t >= SUBLANES:
        if n % t == 0 and t % SUBLANES == 0:
            return t
        t -= SUBLANES
    return n


def _proj_kernel(tn, x_ref, g_ref, w_ref, o_ref):
    x = x_ref[...]
    ms = jnp.mean(x * x, axis=-1, keepdims=True)
    xn = (x * lax.rsqrt(ms + EPS) * g_ref[...]).astype(BF16)
    for c0 in range(0, w_ref.shape[1], tn):
        o_ref[:, c0:c0 + tn] = jnp.dot(xn, w_ref[:, c0:c0 + tn], preferred_element_type=F32)


def _proj(x, g, w):
    m, d = x.shape
    n = w.shape[1]
    tm = _pick(m, 256)
    tn = n // 3 if (n % (3 * LANES) == 0 and n // 3 <= 4096) else _pick(n, 2048)
    return pl.pallas_call(
        functools.partial(_proj_kernel, tn),
        out_shape=jax.ShapeDtypeStruct((m, n), F32),
        grid=(m // tm,),
        in_specs=[pl.BlockSpec((tm, d), lambda i: (i, 0)),
                  pl.BlockSpec((1, d), lambda i: (0, 0)),
                  pl.BlockSpec((d, n), lambda i: (0, 0))],
        out_specs=pl.BlockSpec((tm, n), lambda i: (i, 0)),
        compiler_params=_cparams(("parallel",)),
        name="proj",
    )(x, g, w)


def _rope_kernel(cache_layout, q_ref, k_ref, v_ref, c_ref, sa_ref, sb_ref, *refs):
    if cache_layout:
        refs = refs[2:]
    qo_ref, kb_ref, vb_ref, ko_ref, vo_ref, *ks = refs
    tm = q_ref.shape[0]
    half = ROT_DIM // 2

    def rot(x):
        up = pltpu.roll(x, half, axis=1)
        dn = pltpu.roll(x, LANES - half, axis=1)
        return x * c_ref[...] + dn * sb_ref[...] + up * sa_ref[...]

    k_dst = ks[0] if cache_layout else ko_ref
    for j in range(A_HEADS):
        cs = slice(j * LANES, (j + 1) * LANES)
        qo_ref[:, cs] = (rot(q_ref[:, cs]) * Q_SCALE).astype(BF16)
        k = rot(k_ref[:, cs])
        kb_ref[:, cs] = k.astype(BF16)
        k_dst[:, cs] = k
    vb_ref[...] = v_ref[...].astype(BF16)
    if cache_layout:
        ko_ref[0] = k_dst[...].reshape(tm, A_HEADS, A_DV)
        vo_ref[0] = v_ref[...].reshape(tm, A_HEADS, A_DV)
    else:
        vo_ref[...] = v_ref[...]


def _rope_tables(pos):
    half = ROT_DIM // 2
    f32 = np.float32
    inv = (f32(ROPE_THETA) ** (-np.arange(0, ROT_DIM, 2, dtype=f32) / f32(ROT_DIM))).astype(f32)
    ang = pos.astype(f32)[:, None] * inv[None, :]
    cos, sin = np.cos(ang).astype(f32), np.sin(ang).astype(f32)
    p = pos.shape[0]
    one = np.ones((p, A_DH - ROT_DIM), f32)
    zero = np.zeros((p, A_DH - ROT_DIM), f32)
    zh = np.zeros((p, half), f32)
    c = np.concatenate([cos, cos, one], axis=1)
    sa = np.concatenate([zh, sin, zero], axis=1)
    sb = np.concatenate([-sin, zh, zero], axis=1)
    return tuple(jnp.asarray(np.concatenate([a, a], axis=1)) for a in (c, sa, sb))


def _rope(z, tables, rows_per_table, lead=None):
    m = z.shape[0]
    nb = m // rows_per_table
    n_lead = None if lead is None else lead[0].shape[0]
    tm = _pick(rows_per_table, 512)
    nt = rows_per_table // tm
    zspec = lambda blk: pl.BlockSpec((tm, A_QK), lambda i, blk=blk: (i, blk))
    tspec = pl.BlockSpec((tm, LANES), lambda i: (i % nt, 0))
    ospec = pl.BlockSpec((tm, A_QK), lambda i: (i, 0))
    if n_lead is None:
        fspec, fshape = ospec, jax.ShapeDtypeStruct((m, A_QK), F32)
    else:
        fspec = pl.BlockSpec((pl.Element(1), pl.Element(tm), pl.Element(A_HEADS), pl.Element(A_DV)),
                             lambda i: (i // nt, n_lead + (i % nt) * tm, 0, 0))
        fshape = jax.ShapeDtypeStruct((nb, n_lead + rows_per_table, A_HEADS, A_DV), F32)
    args, in_specs, aliases = [z, z, z, *tables], [zspec(Z_AQ_BLK), zspec(Z_AK_BLK), zspec(Z_AV_BLK), tspec, tspec, tspec], {}
    if lead is not None:
        for a in lead:
            full = jnp.broadcast_to(a[None], (nb,) + a.shape)
            args.append(jnp.pad(full, ((0, 0), (0, rows_per_table), (0, 0), (0, 0))))
            in_specs.append(pl.BlockSpec(memory_space=pl.ANY))
        aliases = {6: 3, 7: 4}
    return pl.pallas_call(
        functools.partial(_rope_kernel, lead is not None),
        out_shape=(jax.ShapeDtypeStruct((m, A_QK), BF16), jax.ShapeDtypeStruct((m, A_QK), BF16),
                   jax.ShapeDtypeStruct((m, A_WIDTH), BF16), fshape, fshape),
        grid=(m // tm,),
        in_specs=in_specs,
        out_specs=(ospec, ospec, ospec, fspec, fspec),
        scratch_shapes=[] if lead is None else [pltpu.VMEM((tm, A_QK), F32)],
        input_output_aliases=aliases,
        compiler_params=_cparams(("parallel",)),
        name="rope",
    )(*args)


def _log_sigmoid(x):
    return -(jnp.maximum(-x, 0.0) + jnp.log1p(jnp.exp(-jnp.abs(x))))


def _sigmoid(x):
    return 1.0 / (1.0 + jnp.exp(-x))


def _dot_hilo(a_bf, x):
    hi = x.astype(BF16)
    lo = (x - hi.astype(F32)).astype(BF16)
    return (jnp.dot(a_bf, hi, preferred_element_type=F32) + jnp.dot(a_bf, lo, preferred_element_type=F32))


def _dot_hilo_r(x, a_bf):
    hi = x.astype(BF16)
    lo = (x - hi.astype(F32)).astype(BF16)
    return (jnp.dot(hi, a_bf, preferred_element_type=F32) + jnp.dot(lo, a_bf, preferred_element_type=F32))


def _mlstm_kernel(n_valid, u_ref, mv_ref, mo_ref, gc_ref, cinit_ref, c0_ref, n0_ref, m0_ref,
                  cw_ref, cb_ref, bc_ref, br_ref, ng_ref,
                  y_ref, co_ref, no_ref, mo_out_ref,
                  ubuf, c_s, n_s, m_s):
    L = u_ref.shape[0]
    c_id = pl.program_id(1)

    @pl.when(c_id == 0)
    def _():
        ubuf[0:SUBLANES, :] = cinit_ref[...]
        c_s[...] = c0_ref[...]
        n_s[...] = n0_ref[...]
        m_s[...] = m0_ref[...]

    ubuf[SUBLANES:SUBLANES + L, :] = u_ref[...]
    y = ubuf[SUBLANES - 3:SUBLANES - 3 + L, :] * cw_ref[0:1, :] + cb_ref[...]
    for j in range(1, CONV_W):
        y = y + ubuf[SUBLANES - 3 + j:SUBLANES - 3 + j + L, :] * cw_ref[j:j + 1, :]
    qk = y * _sigmoid(y)
    ubuf[0:SUBLANES, :] = ubuf[L:L + SUBLANES, :]

    gpre = gc_ref[...]
    gcol = gpre + bc_ref[...]
    grow = gpre.T[0:2 * M_HEADS, :] + br_ref[...]
    li_c, lf_c = gcol, _log_sigmoid(gcol)
    li_r, lf_r = grow, _log_sigmoid(grow)
    if n_valid < L:
        vc = lax.broadcasted_iota(jnp.int32, gcol.shape, 0) < n_valid
        vr = lax.broadcasted_iota(jnp.int32, grow.shape, 1) < n_valid
        li_c, lf_c = jnp.where(vc, li_c, NEG_BIG), jnp.where(vc, lf_c, 0.0)
        li_r, lf_r = jnp.where(vr, li_r, NEG_BIG), jnp.where(vr, lf_r, 0.0)
    t_i = lax.broadcasted_iota(jnp.int32, (L, L), 0)
    s_i = lax.broadcasted_iota(jnp.int32, (L, L), 1)
    tril = s_i <= t_i
    tri_lo = jnp.where(tril, 1.0, 0.0).astype(BF16)
    tri_up = jnp.where(t_i <= s_i, 1.0, 0.0).astype(BF16)
    b_c = _dot_hilo(tri_lo, lf_c)
    b_r = _dot_hilo_r(lf_r, tri_up)

    for h in range(M_HEADS):
        hs = slice(h * M_DK, (h + 1) * M_DK)
        q = qk[:, hs]
        k = qk[:, M_HEADS * M_DK + h * M_DK:M_HEADS * M_DK + (h + 1) * M_DK] * (M_DK ** -0.5)
        v = mv_ref[:, hs]
        qb, kb = q.astype(BF16), k.astype(BF16)
        a_row = li_r[h:h + 1, :] - b_r[M_HEADS + h:M_HEADS + h + 1, :]
        b_col = b_c[:, M_HEADS + h:M_HEADS + h + 1]
        a_col = li_c[:, h:h + 1] - b_col
        m_prev = m_s[h]
        c_prev = c_s[h]
        n_prev = n_s[h]
        a_mat = jnp.where(tril, jnp.broadcast_to(a_row, (L, L)), -jnp.inf)
        g_col = jnp.maximum(m_prev, jnp.max(a_mat, axis=1, keepdims=True))
        d_mat = jnp.exp(a_mat - g_col)
        w_inter = jnp.exp(m_prev - g_col)
        s_mat = lax.dot_general(qb, kb, (((1,), (1,)), ((), ())), preferred_element_type=F32) * d_mat
        cq = lax.dot_general(qb, c_prev.astype(BF16), (((1,), (1,)), ((), ())), preferred_element_type=F32)
        num = jnp.dot(s_mat.astype(BF16), v.astype(BF16), preferred_element_type=F32) + w_inter * cq
        den = jnp.sum(s_mat, axis=1, keepdims=True) + w_inter * jnp.sum(q * n_prev, axis=1, keepdims=True)
        m_t = b_col + g_col
        hh = num / jnp.maximum(jnp.abs(den), jnp.exp(-m_t))
        g_last = g_col[L - 1:L, :]
        w_col = jnp.exp(a_col - g_last)
        decay = jnp.exp(m_prev - g_last)
        vw = (v * w_col).astype(BF16)
        c_s[h] = decay * c_prev + lax.dot_general(vw, kb, (((0,), (0,)), ((), ())), preferred_element_type=F32)
        n_s[h] = decay * n_prev + jnp.sum(k * w_col, axis=0, keepdims=True)
        m_s[h] = b_col[L - 1:L, :] + g_last
        hn = hh * lax.rsqrt(jnp.mean(hh * hh, axis=-1, keepdims=True) + EPS)
        y_ref[:, hs] = (hn * ng_ref[:, hs] * _sigmoid(mo_ref[:, hs])).astype(y_ref.dtype)

    @pl.when(c_id == pl.num_programs(1) - 1)
    def _():
        co_ref[...] = c_s[...]
        no_ref[...] = n_s[...]
        mo_out_ref[...] = m_s[...]


def _mlstm(z, conv_init, c0, n0, m0, prm, nb, nc, n_valid):
    L = CHUNK
    rows = nb * nc * L
    bsel = (lambda b: b) if c0.shape[0] == nb else (lambda b: 0)
    csel = (lambda b: b) if conv_init.shape[0] == nb else (lambda b: 0)
    row = lambda b, c: b * nc + c
    const = lambda shape: pl.BlockSpec(shape, lambda b, c: (0,) * len(shape))
    in_specs = [
        pl.BlockSpec((L, QK_M), lambda b, c: (row(b, c), 0)),
        pl.BlockSpec((L, M_WIDTH), lambda b, c: (row(b, c), Z_MV_BLK)),
        pl.BlockSpec((L, M_WIDTH), lambda b, c: (row(b, c), Z_MO_BLK)),
        pl.BlockSpec((L, LANES), lambda b, c: (row(b, c), Z_GATE_BLK)),
        pl.BlockSpec((None, SUBLANES, QK_M), lambda b, c: (csel(b), 0, 0)),
        pl.BlockSpec((None, M_HEADS, M_DV, M_DK), lambda b, c: (bsel(b), 0, 0, 0)),
        pl.BlockSpec((None, M_HEADS, 1, M_DK), lambda b, c: (bsel(b), 0, 0, 0)),
        pl.BlockSpec((None, M_HEADS, 1, 1), lambda b, c: (bsel(b), 0, 0, 0)),
        const((CONV_W, QK_M)), const((1, QK_M)), const((1, LANES)), const((SUBLANES, L)), const((1, M_WIDTH)),
    ]
    out_specs = (
        pl.BlockSpec((L, M_WIDTH), lambda b, c: (row(b, c), 0)),
        pl.BlockSpec((None, M_HEADS, M_DV, M_DK), lambda b, c: (b, 0, 0, 0)),
        pl.BlockSpec((None, M_HEADS, 1, M_DK), lambda b, c: (b, 0, 0, 0)),
        pl.BlockSpec((None, M_HEADS, 1, 1), lambda b, c: (b, 0, 0, 0)),
    )
    out_shape = (
        jax.ShapeDtypeStruct((rows, M_WIDTH), BF16),
        jax.ShapeDtypeStruct((nb, M_HEADS, M_DV, M_DK), F32),
        jax.ShapeDtypeStruct((nb, M_HEADS, 1, M_DK), F32),
        jax.ShapeDtypeStruct((nb, M_HEADS, 1, 1), F32),
    )
    return pl.pallas_call(
        functools.partial(_mlstm_kernel, n_valid),
        out_shape=out_shape,
        grid=(nb, nc),
        in_specs=in_specs,
        out_specs=out_specs,
        scratch_shapes=[pltpu.VMEM((L + SUBLANES, QK_M), F32), pltpu.VMEM((M_HEADS, M_DV, M_DK), F32),
                        pltpu.VMEM((M_HEADS, 1, M_DK), F32), pltpu.VMEM((M_HEADS, 1, 1), F32)],
        compiler_params=_cparams(("parallel", "arbitrary")),
        name="mlstm",
    )(z, z, z, z, conv_init, c0, n0, m0,
      prm["conv_w"], prm["conv_b"], prm["b_if_row"], prm["b_if_col"], prm["mlstm_g"])


def _mstep_kernel(u_ref, mv_ref, mo_ref, g_ref, sc_ref, c0_ref, n0_ref, m0_ref,
                  cw_ref, cb_ref, bif_ref, ng_ref,
                  y_ref, co_ref, no_ref, mo_out_ref, conv_ref):
    u = u_ref[...]
    sc = sc_ref[...]
    y = sc[0:1, :] * cw_ref[0:1, :] + cb_ref[...]
    y = y + sc[1:2, :] * cw_ref[1:2, :]
    y = y + sc[2:3, :] * cw_ref[2:3, :]
    y = y + u * cw_ref[3:4, :]
    qk = y * _sigmoid(y)
    conv_ref[0:2, :] = sc[1:3, :]
    conv_ref[2:3, :] = u
    gates = g_ref[...] + bif_ref[...]
    lf_all = _log_sigmoid(gates)
    e_i = lax.broadcasted_iota(jnp.int32, (M_DV, M_DV), 0)
    e_j = lax.broadcasted_iota(jnp.int32, (M_DV, M_DV), 1)
    eye = e_i == e_j
    for h in range(M_HEADS):
        hs = slice(h * M_DK, (h + 1) * M_DK)
        q = qk[:, hs]
        k = qk[:, M_HEADS * M_DK + h * M_DK:M_HEADS * M_DK + (h + 1) * M_DK] * (M_DK ** -0.5)
        v = mv_ref[:, hs]
        li = gates[:, h:h + 1]
        lf = lf_all[:, M_HEADS + h:M_HEADS + h + 1]
        m_prev = m0_ref[h]
        c_prev = c0_ref[h]
        n_prev = n0_ref[h]
        m_t = jnp.maximum(lf + m_prev, li)
        d_w = jnp.exp(li - m_t)
        w_inter = jnp.exp(lf + m_prev - m_t)
        s = jnp.sum(q * k, axis=1, keepdims=True) * d_w
        q8 = jnp.broadcast_to(q, (SUBLANES, M_DK)).astype(BF16)
        cq = lax.dot_general(q8, c_prev.astype(BF16), (((1,), (1,)), ((), ())), preferred_element_type=F32)[0:1, :]
        num = s * v + w_inter * cq
        den = s + w_inter * jnp.sum(q * n_prev, axis=1, keepdims=True)
        hh = num / jnp.maximum(jnp.abs(den), jnp.exp(-m_t))
        vdiag = jnp.where(eye, jnp.broadcast_to(v * d_w, (M_DV, M_DV)), 0.0).astype(BF16)
        krows = jnp.broadcast_to(k, (M_DV, M_DK)).astype(BF16)
        co_ref[h] = w_inter * c_prev + jnp.dot(vdiag, krows, preferred_element_type=F32)
        no_ref[h] = w_inter * n_prev + d_w * k
        mo_out_ref[h] = m_t
        hn = hh * lax.rsqrt(jnp.mean(hh * hh, axis=-1, keepdims=True) + EPS)
        y_ref[:, hs] = (hn * ng_ref[:, hs] * _sigmoid(mo_ref[:, hs])).astype(y_ref.dtype)


def _mstep(z3, state_conv, c0, n0, m0, prm):
    nb = z3.shape[0]
    const = lambda shape: pl.BlockSpec(shape, lambda b: (0,) * len(shape))
    zspec = lambda w, blk: pl.BlockSpec((None, 1, w), lambda b, blk=blk: (b, 0, blk))
    st = lambda shape: pl.BlockSpec((None,) + shape, lambda b: (b,) + (0,) * len(shape))
    return pl.pallas_call(
        _mstep_kernel,
        out_shape=(jax.ShapeDtypeStruct((nb, 1, M_WIDTH), BF16),
                   jax.ShapeDtypeStruct((nb, M_HEADS, M_DV, M_DK), F32),
                   jax.ShapeDtypeStruct((nb, M_HEADS, 1, M_DK), F32),
                   jax.ShapeDtypeStruct((nb, M_HEADS, 1, 1), F32),
                   jax.ShapeDtypeStruct((nb, CONV_W - 1, QK_M), F32)),
        grid=(nb,),
        in_specs=[zspec(QK_M, 0), zspec(M_WIDTH, Z_MV_BLK), zspec(M_WIDTH, Z_MO_BLK), zspec(LANES, Z_GATE_BLK),
                  st((CONV_W - 1, QK_M)), st((M_HEADS, M_DV, M_DK)), st((M_HEADS, 1, M_DK)), st((M_HEADS, 1, 1)),
                  const((CONV_W, QK_M)), const((1, QK_M)), const((1, LANES)), const((1, M_WIDTH))],
        out_specs=(pl.BlockSpec((None, 1, M_WIDTH), lambda b: (b, 0, 0)),
                   st((M_HEADS, M_DV, M_DK)), st((M_HEADS, 1, M_DK)), st((M_HEADS, 1, 1)), st((CONV_W - 1, QK_M))),
        compiler_params=_cparams(("parallel",)),
        name="mstep",
    )(z3, z3, z3, z3, state_conv, c0, n0, m0,
      prm["conv_w"], prm["conv_b"], prm["b_if_row"], prm["mlstm_g"])


def _diff_lambda(l_ref):
    a = jnp.sum(l_ref[0:1, :] * l_ref[1:2, :], axis=1, keepdims=True)
    b = jnp.sum(l_ref[2:3, :] * l_ref[3:4, :], axis=1, keepdims=True)
    return jnp.exp(a) - jnp.exp(b) + LAM_INIT


def _class_allreduce(x, op):
    for sh in (A_HEADS, 2 * A_HEADS, 4 * A_HEADS, 8 * A_HEADS):
        x = op(x, pltpu.roll(x, sh, axis=1))
    return x


def _decode_update(kmat, vmat, n_lanes_valid, m_s, l_s, acc_s):
    nt = (((1,), (1,)), ((), ()))
    r_i = lax.broadcasted_iota(jnp.int32, (SUBLANES, LANES), 0)
    l_i = lax.broadcasted_iota(jnp.int32, (SUBLANES, LANES), 1)
    rsel = jnp.where((r_i < 2) & ((l_i >= A_DH) == (r_i == 1)), 1.0, 0.0).astype(BF16)
    diag = (l_i % A_HEADS) == r_i
    rows = kmat.shape[0]
    st = lax.dot_general(rsel, kmat.astype(BF16), nt, preferred_element_type=F32)
    tiles = [st[:, t * LANES:(t + 1) * LANES] for t in range(rows // LANES)]
    if n_lanes_valid is not None:
        tiles = [jnp.where(l_i + t * LANES < n_lanes_valid, x, -jnp.inf) for t, x in enumerate(tiles)]
    mb = tiles[0]
    for x in tiles[1:]:
        mb = jnp.maximum(mb, x)
    mb = _class_allreduce(mb, jnp.maximum)
    m_old = m_s[...]
    m_new = jnp.maximum(m_old, mb)
    alpha = jnp.exp2(m_old - m_new)
    ps = [jnp.exp2(x - m_new) for x in tiles]
    lsum = ps[0]
    for x in ps[1:]:
        lsum = lsum + x
    l_s[...] = alpha * l_s[...] + lsum
    m_s[...] = m_new
    pm = []
    for c in range(2):
        pm.append(jnp.concatenate(
            [jnp.where(diag, jnp.broadcast_to(x[c:c + 1, :], (SUBLANES, LANES)), 0.0) for x in ps], axis=1))
    pm = jnp.concatenate(pm, axis=0).astype(BF16)
    pv = jnp.dot(pm, vmat.astype(BF16), preferred_element_type=F32)
    acc_s[...] = _decode_colvec(alpha) * acc_s[...] + pv


def _decode_colvec(x_rep):
    r_i = lax.broadcasted_iota(jnp.int32, (SUBLANES, LANES), 0)
    l_i = lax.broadcasted_iota(jnp.int32, (SUBLANES, LANES), 1)
    cols = []
    for c in range(2):
        xb = jnp.broadcast_to(x_rep[c:c + 1, :], (SUBLANES, LANES))
        cols.append(jnp.sum(jnp.where(l_i == r_i, xb, 0.0), axis=1, keepdims=True))
    return jnp.concatenate(cols, axis=0)


def _attn_decode_kernel(n_meta, rb, nq, g_pages, n_dec, dps, spp,
                        pt_ref, q_ref, k_ref, v_ref, km_ref, vm_ref, lam_ref, ag_ref, qd_ref, kn_ref, vn_ref,
                        ck_ref, cv_ref, o_ref, od_ref,
                        qq_s, mx_s, acc_s, s_s, sd_s, kbuf, vbuf, sem, dm_s, dl_s, dacc_s):
    tq = q_ref.shape[0]
    b, h, i = pl.program_id(0), pl.program_id(1), pl.program_id(2)
    step = (b * pl.num_programs(1) + h) * pl.num_programs(2) + i

    def page_copies(d):
        slot, sb, pg = lax.rem(d, 2), lax.div(d, spp), lax.rem(d, spp)
        cps = []
        for t in range(g_pages):
            pidx = pt_ref[sb, pg * g_pages + t]
            cps.append(pltpu.make_async_copy(ck_ref.at[pidx], kbuf.at[slot, t], sem.at[0, slot]))
            cps.append(pltpu.make_async_copy(cv_ref.at[pidx], vbuf.at[slot, t], sem.at[1, slot]))
        return cps

    def issue(d):
        @pl.when(d < n_dec)
        def _():
            for cp in page_copies(d):
                cp.start()

    def decode_step(d):
        @pl.when(d < n_dec)
        def _():
            for cp in page_copies(d):
                cp.wait()
            slot, sb, pg = lax.rem(d, 2), lax.div(d, spp), lax.rem(d, spp)
            q = qd_ref[sb]

            @pl.when(pg == 0)
            def _():
                dm_s[...] = jnp.full(dm_s.shape, NEG_BIG, F32)
                dl_s[...] = jnp.zeros(dl_s.shape, F32)
                dacc_s[...] = jnp.zeros(dacc_s.shape, F32)
                pad = jnp.zeros((LANES - SUBLANES, LANES), F32)
                _decode_update(jnp.concatenate([kn_ref[sb] * q, pad], axis=0),
                               jnp.concatenate([vn_ref[sb], pad], axis=0), A_HEADS, dm_s, dl_s, dacc_s)

            kmat = jnp.concatenate([(kbuf[slot, t, :, 0] * q[None, :, :]).reshape(-1, LANES) for t in range(g_pages)],
                                   axis=0)
            vmat = jnp.concatenate([vbuf[slot, t, :, 0].reshape(-1, LANES) for t in range(g_pages)], axis=0)
            _decode_update(kmat, vmat, None, dm_s, dl_s, dacc_s)

            @pl.when(pg == spp - 1)
            def _():
                lam = _diff_lambda(lam_ref)
                l_rep = _class_allreduce(dl_s[...], jnp.add)
                o = dacc_s[...] / _decode_colvec(l_rep)
                o = o[0:A_HEADS, :] - lam * o[A_HEADS:2 * A_HEADS, :]
                on = o * lax.rsqrt(jnp.mean(o * o, axis=-1, keepdims=True) + EPS)
                od_ref[sb] = (on * ag_ref[...] * (1.0 - LAM_INIT)).astype(od_ref.dtype)

    d0 = step * dps

    @pl.when(step == 0)
    def _():
        issue(0)

    issue(d0 + 1)

    q = q_ref[...]
    lane = lax.broadcasted_iota(jnp.int32, q.shape, 1)
    zero = jnp.zeros_like(q)
    qq_s[0:tq, :] = jnp.where(lane < A_DH, q, zero)
    qq_s[tq:2 * tq, :] = jnp.where(lane >= A_DH, q, zero)
    nt = (((1,), (1,)), ((), ()))
    blocks = [(r0, r0 % tq) for r0 in range(0, 2 * tq, rb)]

    def scores(r0, kc):
        return lax.dot_general(qq_s[r0:r0 + rb, :], kc, nt, preferred_element_type=F32)

    def fold_max(m, s):
        for t in range(s.shape[1] // LANES):
            m = jnp.maximum(m, s[:, t * LANES:(t + 1) * LANES])
        return m

    def probs(s, m_rep):
        return jnp.exp2(s - jnp.concatenate([m_rep] * (s.shape[1] // LANES), axis=1)).astype(BF16)

    def with_ones(v):
        return jnp.concatenate([v, jnp.ones((v.shape[0], LANES), v.dtype)], axis=1)

    def pass1_chunk(j, width):
        kc = k_ref[pl.ds(pl.multiple_of(j * tq, tq), width), :]
        for r0, _ in blocks:
            s = scores(r0, kc)
            for w in range(width // tq):
                s_s[j + w, r0:r0 + rb, :] = s[:, w * tq:(w + 1) * tq]
            mx_s[r0:r0 + rb, :] = fold_max(mx_s[r0:r0 + rb, :], s)

    def pass2_chunk(j, width):
        v1 = with_ones(v_ref[pl.ds(pl.multiple_of(j * tq, tq), width), :])
        for r0, _ in blocks:
            s = jnp.concatenate([s_s[j + w, r0:r0 + rb, :] for w in range(width // tq)], axis=1)
            acc_s[r0:r0 + rb, :] += jnp.dot(probs(s, mx_s[r0:r0 + rb, :]), v1, preferred_element_type=F32)

    def full_chunks(chunk_fn):
        if nq >= 2:
            def pair(jj, carry):
                chunk_fn(2 * jj, 2 * tq)
                return carry

            lax.fori_loop(0, lax.shift_right_logical(i, 1), pair, 0)

            @pl.when(lax.rem(i, 2) == 1)
            def _():
                chunk_fn(i - 1, tq)

    dstart = pl.multiple_of(i * tq, tq)

    for r0, q0 in blocks:
        n = q0 + rb
        s = scores(r0, jnp.concatenate([km_ref[...], k_ref[pl.ds(dstart, n), :]], axis=0))
        row = lax.broadcasted_iota(jnp.int32, s.shape, 0) + q0
        col = lax.broadcasted_iota(jnp.int32, s.shape, 1)
        s = jnp.where((col < n_meta) | ((col >= LANES) & (col - LANES <= row)), s, -jnp.inf)
        sd_s[r0:r0 + rb, 0:LANES + n] = s
        mx_s[r0:r0 + rb, :] = fold_max(s[:, 0:LANES], s[:, LANES:])
    full_chunks(pass1_chunk)
    for r0, _ in blocks:
        m = jnp.max(mx_s[r0:r0 + rb, :], axis=1, keepdims=True)
        mx_s[r0:r0 + rb, :] = jnp.broadcast_to(m, (rb, LANES))

    for r0, q0 in blocks:
        n = q0 + rb
        p = probs(sd_s[r0:r0 + rb, 0:LANES + n], mx_s[r0:r0 + rb, :])
        v1 = with_ones(jnp.concatenate([vm_ref[...], v_ref[pl.ds(dstart, n), :]], axis=0))
        acc_s[r0:r0 + rb, :] = jnp.dot(p, v1, preferred_element_type=F32)
    full_chunks(pass2_chunk)

    o0 = acc_s[0:tq, 0:A_DV] / acc_s[0:tq, A_DV:2 * A_DV]
    o1 = acc_s[tq:2 * tq, 0:A_DV] / acc_s[tq:2 * tq, A_DV:2 * A_DV]
    o = o0 - _diff_lambda(lam_ref) * o1
    on = o * lax.rsqrt(jnp.mean(o * o, axis=-1, keepdims=True) + EPS)
    o_ref[...] = (on * ag_ref[...] * (1.0 - LAM_INIT)).astype(o_ref.dtype)

    decode_step(d0)
    for u in range(1, dps):
        issue(d0 + u + 1)
        decode_step(d0 + u)


def _attn_decode(q, k, v, k_meta, v_meta, n_meta, qd, k_new, v_new, cache_k, cache_v, page_table, lam4, attn_g):
    bsz, t, _ = q.shape
    db, n_pages = page_table.shape
    page = cache_k.shape[1]
    tq = _pick(t, 512)
    rb = _pick(tq, 256)
    nq = t // tq
    g_pages = 8 if n_pages % 8 == 0 else (2 if n_pages % 2 == 0 else 1)
    spp = n_pages // g_pages
    n_dec = db * spp
    n_steps = bsz * A_HEADS * nq
    dps = -(-n_dec // n_steps)
    hspec = lambda rows: pl.BlockSpec((None, rows, LANES), lambda b, h, i, pt: (b, 0, h))
    const = lambda shape: pl.BlockSpec(shape, lambda b, h, i, pt: (0,) * len(shape))
    qblk = pl.BlockSpec((None, tq, LANES), lambda b, h, i, pt: (b, i, h))
    grid_spec = pltpu.PrefetchScalarGridSpec(
        num_scalar_prefetch=1,
        grid=(bsz, A_HEADS, nq),
        in_specs=[qblk, hspec(t), hspec(t),
                  pl.BlockSpec((LANES, LANES), lambda b, h, i, pt: (0, h)),
                  pl.BlockSpec((LANES, LANES), lambda b, h, i, pt: (0, h)),
                  const((4, A_DH)), const((1, A_DV)),
                  const((db, A_HEADS, A_DV)), const((db, A_HEADS, A_DV)), const((db, A_HEADS, A_DV)),
                  pl.BlockSpec(memory_space=pl.ANY), pl.BlockSpec(memory_space=pl.ANY)],
        out_specs=(qblk, const((db, A_HEADS, A_DV))),
        scratch_shapes=[pltpu.VMEM((2 * tq, LANES), BF16), pltpu.VMEM((2 * tq, LANES), F32),
                        pltpu.VMEM((2 * tq, 2 * A_DV), F32), pltpu.VMEM((max(nq - 1, 1), 2 * tq, tq), F32),
                        pltpu.VMEM((2 * tq, LANES + tq), F32),
                        pltpu.VMEM((2, g_pages, page, 1, A_HEADS, A_DV), F32),
                        pltpu.VMEM((2, g_pages, page, 1, A_HEADS, A_DV), F32),
                        pltpu.SemaphoreType.DMA((2, 2)),
                        pltpu.VMEM((SUBLANES, LANES), F32), pltpu.VMEM((SUBLANES, LANES), F32),
                        pltpu.VMEM((2 * A_HEADS, A_DV), F32)],
    )
    return pl.pallas_call(
        functools.partial(_attn_decode_kernel, n_meta, rb, nq, g_pages, n_dec, dps, spp),
        out_shape=(jax.ShapeDtypeStruct((bsz, t, A_WIDTH), BF16), jax.ShapeDtypeStruct((db, A_HEADS, A_DV), BF16)),
        grid_spec=grid_spec,
        compiler_params=_cparams(("arbitrary", "arbitrary", "arbitrary")),
        name="attn_decode",
    )(page_table, q, k, v, k_meta, v_meta, lam4, attn_g, qd, k_new, v_new, cache_k, cache_v)


def _post_kernel(x_ref, ym_ref, ya_ref, ga_ref, gb_ref, wpa_ref, wpb_ref, wo_ref, g1_ref, wu_ref, wd_ref, g2_ref, o_ref):
    pa = jnp.dot(ym_ref[...], wpa_ref[...], preferred_element_type=F32)
    pb = jnp.dot(ya_ref[...], wpb_ref[...], preferred_element_type=F32)
    mixed = _sigmoid(ga_ref[...]) * pa + _sigmoid(gb_ref[...]) * pb
    x = x_ref[...] + jnp.dot(mixed.astype(BF16), wo_ref[...], preferred_element_type=F32)
    xn = (x * lax.rsqrt(jnp.mean(x * x, axis=-1, keepdims=True) + EPS) * g1_ref[...]).astype(BF16)
    acc = x
    fc = D_MODEL
    for f in range(D_FF // fc):
        hf = jnp.dot(xn, wu_ref[:, f * fc:(f + 1) * fc], preferred_element_type=F32)
        hf = jnp.square(jnp.maximum(hf, 0.0)).astype(BF16)
        acc = acc + jnp.dot(hf, wd_ref[f * fc:(f + 1) * fc, :], preferred_element_type=F32)
    o_ref[...] = acc * lax.rsqrt(jnp.mean(acc * acc, axis=-1, keepdims=True) + EPS) * g2_ref[...]


def _post(x, y_m, y_a, z, prm):
    m = x.shape[0]
    tm = _pick(m, 512)
    rspec = lambda w, blk=0: pl.BlockSpec((tm, w), lambda i, blk=blk: (i, blk))
    const = lambda shape: pl.BlockSpec(shape, lambda i: (0, 0), pipeline_mode=pl.Buffered(1))
    return pl.pallas_call(
        _post_kernel,
        out_shape=jax.ShapeDtypeStruct((m, D_MODEL), F32),
        grid=(m // tm,),
        in_specs=[rspec(D_MODEL), rspec(M_WIDTH), rspec(A_WIDTH), rspec(D_MODEL, Z_GA_BLK), rspec(D_MODEL, Z_GB_BLK),
                  const((M_WIDTH, D_MODEL)), const((A_WIDTH, D_MODEL)), const((D_MODEL, D_MODEL)),
                  const((1, D_MODEL)), const((D_MODEL, D_FF)), const((D_FF, D_MODEL)), const((1, D_MODEL))],
        out_specs=rspec(D_MODEL),
        compiler_params=_cparams(("parallel",)),
        name="post",
    )(x, y_m, y_a, z, z, prm["w_pa"], prm["w_pb"], prm["w_out"], prm["ffn_g"], prm["w_up"], prm["w_down"], prm["final_g"])


def kernel(x_prompt, x_sample, cache_k, cache_v, state_C, state_n, state_m, state_conv, page_table, meta_tokens, norm_mix_g, norm_ffn_g, w_in, b_if, conv_w, conv_b, mlstm_norm_g, lambda_q1, lambda_k1, lambda_q2, lambda_k2, attn_norm_g, w_proj_a, w_proj_b, w_out, w_up, w_down, final_norm_g):
    bsz, seq, d = x_prompt.shape
    dbsz, s_dec, _ = x_sample.shape
    n_pages = page_table.shape[1]
    past = n_pages * cache_k.shape[1]
    assert d == D_MODEL and s_dec == 1 and seq % CHUNK == 0 and norm_mix_g.shape[0] == 1
    assert N_META + dbsz <= CHUNK
    l = 0

    gate_lo = QK_M + 2 * M_WIDTH
    gate_hi = gate_lo + 2 * M_HEADS
    w = w_in[l]
    w_re = jnp.concatenate([w[:, :gate_lo], w[:, gate_hi:], w[:, gate_lo:gate_hi],
                            jnp.zeros((d, LANES - 2 * M_HEADS), w.dtype)], axis=1).astype(BF16)
    bif = b_if[l].astype(F32)
    prm = {
        "conv_w": conv_w[l].astype(F32), "conv_b": conv_b[l].astype(F32)[None, :],
        "b_if_row": jnp.pad(bif, (0, LANES - 2 * M_HEADS))[None, :],
        "b_if_col": jnp.broadcast_to(bif[:, None], (2 * M_HEADS, CHUNK)),
        "mlstm_g": mlstm_norm_g[l].astype(F32)[None, :],
        "w_pa": w_proj_a[l].astype(BF16), "w_pb": w_proj_b[l].astype(BF16), "w_out": w_out[l].astype(BF16),
        "ffn_g": norm_ffn_g[l].astype(F32)[None, :], "w_up": w_up[l].astype(BF16), "w_down": w_down[l].astype(BF16),
        "final_g": final_norm_g.astype(F32)[None, :],
    }
    mix_g = norm_mix_g[l].astype(F32)[None, :]
    lam4 = jnp.stack([lambda_q1[l], lambda_k1[l], lambda_q2[l], lambda_k2[l]]).astype(F32)
    attn_g = attn_norm_g[l].astype(F32)[None, :]

    xp = x_prompt.reshape(bsz * seq, d)
    xs = x_sample.reshape(dbsz, d)
    xe = jnp.concatenate([meta_tokens.astype(F32), xs, jnp.zeros((CHUNK - N_META - dbsz, d), F32)], axis=0)
    z_p = _proj(xp, mix_g, w_re)
    z_e = _proj(xe, mix_g, w_re)

    pos_e = np.concatenate([np.arange(N_META), np.full((dbsz,), past), np.zeros((CHUNK - N_META - dbsz,), np.int64)])
    q_e, kb_e, vb_e, k_e, v_e = _rope(z_e, _rope_tables(pos_e), CHUNK)
    heads = lambda a: a.reshape(a.shape[0], A_HEADS, A_DV)
    q_p, kb_p, vb_p, k_p, v_p = _rope(z_p, _rope_tables(N_META + np.arange(seq)), seq,
                                      (heads(k_e[:N_META]), heads(v_e[:N_META])))

    zero_state = (jnp.zeros((1, M_HEADS, M_DV, M_DK), F32), jnp.zeros((1, M_HEADS, 1, M_DK), F32),
                  jnp.zeros((1, M_HEADS, 1, 1), F32))
    _, c_m, n_m, m_m = _mlstm(z_e, jnp.zeros((1, SUBLANES, QK_M), F32), *zero_state, prm, 1, 1, N_META)
    conv_init = z_e[N_META - SUBLANES:N_META, :QK_M][None]
    ym_p, c_p, n_p, m_p = _mlstm(z_p, conv_init, c_m, n_m, m_m, prm, bsz, seq // CHUNK, CHUNK)
    z_s = z_e[N_META:N_META + dbsz]
    ym_s, c_s, n_s, m_s, conv_s = _mstep(
        z_s[:, None, :], state_conv[l].astype(F32), state_C[l].astype(F32),
        state_n[l].astype(F32)[:, :, None, :], state_m[l].astype(F32)[:, :, None, None], prm)

    hv = lambda a: a[N_META:N_META + dbsz].reshape(dbsz, A_HEADS, A_DV)
    v_s = v_e[N_META:N_META + dbsz]
    ya_p, ya_s = _attn_decode(
        q_p.reshape(bsz, seq, A_QK), kb_p.reshape(bsz, seq, A_QK), vb_p.reshape(bsz, seq, A_WIDTH), kb_e, vb_e, N_META,
        hv(q_e).astype(F32), hv(k_e), v_s.reshape(dbsz, A_HEADS, A_DV), cache_k, cache_v,
        page_table.astype(jnp.int32), lam4, attn_g)

    y_p = _post(xp, ym_p, ya_p.reshape(bsz * seq, A_WIDTH), z_p, prm)
    y_s = _post(xs, ym_s.reshape(dbsz, M_WIDTH), ya_s.reshape(dbsz, A_WIDTH), z_s, prm)

    k_prompt = k_p.reshape(bsz, N_META + seq, 1, A_HEADS, A_DV)
    v_prompt = v_p.reshape(bsz, N_META + seq, 1, A_HEADS, A_DV)
    conv_prompt = z_p.reshape(bsz, seq, Z_COLS)[:, seq - (CONV_W - 1):, :QK_M][None]
    return (y_p.reshape(bsz, seq, d), y_s.reshape(dbsz, 1, d),
            k_prompt, v_prompt,
            c_p[None], n_p.reshape(1, bsz, M_HEADS, M_DK), m_p.reshape(1, bsz, M_HEADS), conv_prompt,
            k_e[N_META:N_META + dbsz].reshape(dbsz, 1, 1, A_HEADS, A_DV), v_s.reshape(dbsz, 1, 1, A_HEADS, A_DV),
            c_s[None], n_s.reshape(1, dbsz, M_HEADS, M_DK), m_s.reshape(1, dbsz, M_HEADS), conv_s[None])
```

```python
import functools
import math

import jax
import jax.numpy as jnp
import numpy as np
from jax import lax
from jax.experimental import pallas as pl
from jax.experimental.pallas import tpu as pltpu

F32 = jnp.float32
BF16 = jnp.bfloat16

D_MODEL = 1024
N_META = 16
M_HEADS = 4
M_DK = 128
M_DV = 128
M_WIDTH = M_HEADS * M_DV
QK_M = 2 * M_HEADS * M_DK
CONV_W = 4
NEG_BIG = -1e30
A_HEADS = 8
A_DH = 64
A_DV = 2 * A_DH
A_QK = A_HEADS * 2 * A_DH
A_WIDTH = A_HEADS * A_DV
ROT_DIM = A_DH // 4
ROPE_THETA = 500000.0
D_FF = 4 * D_MODEL
EPS = 1e-6
LAM_INIT = 0.8 - 0.6 * math.exp(-0.3 * 0)
Q_SCALE = (A_DH ** -0.5) * math.log2(math.e)

LANES = 128
SUBLANES = 8
VMEM_LIMIT_BYTES = 56 * 1024 * 1024

Z_QK = 0
Z_MV_BLK = 2
Z_MO_BLK = 3
Z_AQ_BLK = 2
Z_AK_BLK = 3
Z_AV_BLK = 4
Z_GA_BLK = 5
Z_GB_BLK = 6
Z_GATE_COL = 7 * 1024
Z_GATE_BLK = Z_GATE_COL // LANES
Z_COLS = Z_GATE_COL + LANES

CHUNK = 256


def _cparams(sem):
    return pltpu.CompilerParams(dimension_semantics=sem, vmem_limit_bytes=VMEM_LIMIT_BYTES)


def _pick(n, target):
    if n <= target:
        return n
    t = target
    while t >= SUBLANES:
        if n % t == 0 and t % SUBLANES == 0:
            return t
        t -= SUBLANES
    return n


def _proj_kernel(tn, x_ref, g_ref, w_ref, o_ref):
    x = x_ref[...]
    ms = jnp.mean(x * x, axis=-1, keepdims=True)
    xn = (x * lax.rsqrt(ms + EPS) * g_ref[...]).astype(BF16)
    for c0 in range(0, w_ref.shape[1], tn):
        o_ref[:, c0:c0 + tn] = jnp.dot(xn, w_ref[:, c0:c0 + tn], preferred_element_type=F32)


def _proj(x, g, w):
    m, d = x.shape
    n = w.shape[1]
    tm = _pick(m, 256)
    tn = n // 3 if (n % (3 * LANES) == 0 and n // 3 <= 4096) else _pick(n, 2048)
    return pl.pallas_call(
        functools.partial(_proj_kernel, tn),
        out_shape=jax.ShapeDtypeStruct((m, n), F32),
        grid=(m // tm,),
        in_specs=[pl.BlockSpec((tm, d), lambda i: (i, 0)),
                  pl.BlockSpec((1, d), lambda i: (0, 0)),
                  pl.BlockSpec((d, n), lambda i: (0, 0))],
        out_specs=pl.BlockSpec((tm, n), lambda i: (i, 0)),
        compiler_params=_cparams(("parallel",)),
        name="proj",
    )(x, g, w)


def _rope_kernel(cache_layout, q_ref, k_ref, v_ref, c_ref, sa_ref, sb_ref, *refs):
    if cache_layout:
        refs = refs[2:]
    qo_ref, kb_ref, vb_ref, ko_ref, vo_ref, *ks = refs
    tm = q_ref.shape[0]
    half = ROT_DIM // 2

    def rot(x):
        up = pltpu.roll(x, half, axis=1)
        dn = pltpu.roll(x, LANES - half, axis=1)
        return x * c_ref[...] + dn * sb_ref[...] + up * sa_ref[...]

    k_dst = ks[0] if cache_layout else ko_ref
    for j in range(A_HEADS):
        cs = slice(j * LANES, (j + 1) * LANES)
        qo_ref[:, cs] = (rot(q_ref[:, cs]) * Q_SCALE).astype(BF16)
        k = rot(k_ref[:, cs])
        kb_ref[:, cs] = k.astype(BF16)
        k_dst[:, cs] = k
    vb_ref[...] = v_ref[...].astype(BF16)
    if cache_layout:
        ko_ref[0] = k_dst[...].reshape(tm, A_HEADS, A_DV)
        vo_ref[0] = v_ref[...].reshape(tm, A_HEADS, A_DV)
    else:
        vo_ref[...] = v_ref[...]


def _rope_tables(pos):
    half = ROT_DIM // 2
    f32 = np.float32
    inv = (f32(ROPE_THETA) ** (-np.arange(0, ROT_DIM, 2, dtype=f32) / f32(ROT_DIM))).astype(f32)
    ang = pos.astype(f32)[:, None] * inv[None, :]
    cos, sin = np.cos(ang).astype(f32), np.sin(ang).astype(f32)
    p = pos.shape[0]
    one = np.ones((p, A_DH - ROT_DIM), f32)
    zero = np.zeros((p, A_DH - ROT_DIM), f32)
    zh = np.zeros((p, half), f32)
    c = np.concatenate([cos, cos, one], axis=1)
    sa = np.concatenate([zh, sin, zero], axis=1)
    sb = np.concatenate([-sin, zh, zero], axis=1)
    return tuple(jnp.asarray(np.concatenate([a, a], axis=1)) for a in (c, sa, sb))


def _rope(z, tables, rows_per_table, lead=None):
    m = z.shape[0]
    nb = m // rows_per_table
    n_lead = None if lead is None else lead[0].shape[0]
    tm = _pick(rows_per_table, 512)
    nt = rows_per_table // tm
    zspec = lambda blk: pl.BlockSpec((tm, A_QK), lambda i, blk=blk: (i, blk))
    tspec = pl.BlockSpec((tm, LANES), lambda i: (i % nt, 0))
    ospec = pl.BlockSpec((tm, A_QK), lambda i: (i, 0))
    if n_lead is None:
        fspec, fshape = ospec, jax.ShapeDtypeStruct((m, A_QK), F32)
    else:
        fspec = pl.BlockSpec((pl.Element(1), pl.Element(tm), pl.Element(A_HEADS), pl.Element(A_DV)),
                             lambda i: (i // nt, n_lead + (i % nt) * tm, 0, 0))
        fshape = jax.ShapeDtypeStruct((nb, n_lead + rows_per_table, A_HEADS, A_DV), F32)
    args, in_specs, aliases = [z, z, z, *tables], [zspec(Z_AQ_BLK), zspec(Z_AK_BLK), zspec(Z_AV_BLK), tspec, tspec, tspec], {}
    if lead is not None:
        for a in lead:
            full = jnp.broadcast_to(a[None], (nb,) + a.shape)
            args.append(jnp.pad(full, ((0, 0), (0, rows_per_table), (0, 0), (0, 0))))
            in_specs.append(pl.BlockSpec(memory_space=pl.ANY))
        aliases = {6: 3, 7: 4}
    return pl.pallas_call(
        functools.partial(_rope_kernel, lead is not None),
        out_shape=(jax.ShapeDtypeStruct((m, A_QK), BF16), jax.ShapeDtypeStruct((m, A_QK), BF16),
                   jax.ShapeDtypeStruct((m, A_WIDTH), BF16), fshape, fshape),
        grid=(m // tm,),
        in_specs=in_specs,
        out_specs=(ospec, ospec, ospec, fspec, fspec),
        scratch_shapes=[] if lead is None else [pltpu.VMEM((tm, A_QK), F32)],
        input_output_aliases=aliases,
        compiler_params=_cparams(("parallel",)),
        name="rope",
    )(*args)


def _log_sigmoid(x):
    return -(jnp.maximum(-x, 0.0) + jnp.log1p(jnp.exp(-jnp.abs(x))))


def _sigmoid(x):
    return 1.0 / (1.0 + jnp.exp(-x))


def _dot_hilo(a_bf, x):
    hi = x.astype(BF16)
    lo = (x - hi.astype(F32)).astype(BF16)
    return (jnp.dot(a_bf, hi, preferred_element_type=F32) + jnp.dot(a_bf, lo, preferred_element_type=F32))


def _dot_hilo_r(x, a_bf):
    hi = x.astype(BF16)
    lo = (x - hi.astype(F32)).astype(BF16)
    return (jnp.dot(hi, a_bf, preferred_element_type=F32) + jnp.dot(lo, a_bf, preferred_element_type=F32))


def _mlstm_kernel(n_valid, u_ref, mv_ref, mo_ref, gc_ref, cinit_ref, c0_ref, n0_ref, m0_ref,
                  cw_ref, cb_ref, bc_ref, br_ref, ng_ref,
                  y_ref, co_ref, no_ref, mo_out_ref,
                  ubuf, c_s, n_s, m_s):
    L = u_ref.shape[0]
    c_id = pl.program_id(1)

    @pl.when(c_id == 0)
    def _():
        ubuf[0:SUBLANES, :] = cinit_ref[...]
        c_s[...] = c0_ref[...]
        n_s[...] = n0_ref[...]
        m_s[...] = m0_ref[...]

    ubuf[SUBLANES:SUBLANES + L, :] = u_ref[...]
    y = ubuf[SUBLANES - 3:SUBLANES - 3 + L, :] * cw_ref[0:1, :] + cb_ref[...]
    for j in range(1, CONV_W):
        y = y + ubuf[SUBLANES - 3 + j:SUBLANES - 3 + j + L, :] * cw_ref[j:j + 1, :]
    qk = y * _sigmoid(y)
    ubuf[0:SUBLANES, :] = ubuf[L:L + SUBLANES, :]

    gpre = gc_ref[...]
    gcol = gpre + bc_ref[...]
    grow = gpre.T[0:2 * M_HEADS, :] + br_ref[...]
    li_c, lf_c = gcol, _log_sigmoid(gcol)
    li_r, lf_r = grow, _log_sigmoid(grow)
    if n_valid < L:
        vc = lax.broadcasted_iota(jnp.int32, gcol.shape, 0) < n_valid
        vr = lax.broadcasted_iota(jnp.int32, grow.shape, 1) < n_valid
        li_c, lf_c = jnp.where(vc, li_c, NEG_BIG), jnp.where(vc, lf_c, 0.0)
        li_r, lf_r = jnp.where(vr, li_r, NEG_BIG), jnp.where(vr, lf_r, 0.0)
    t_i = lax.broadcasted_iota(jnp.int32, (L, L), 0)
    s_i = lax.broadcasted_iota(jnp.int32, (L, L), 1)
    tril = s_i <= t_i
    tri_lo = jnp.where(tril, 1.0, 0.0).astype(BF16)
    tri_up = jnp.where(t_i <= s_i, 1.0, 0.0).astype(BF16)
    b_c = _dot_hilo(tri_lo, lf_c)
    b_r = _dot_hilo_r(lf_r, tri_up)

    for h in range(M_HEADS):
        hs = slice(h * M_DK, (h + 1) * M_DK)
        q = qk[:, hs]
        k = qk[:, M_HEADS * M_DK + h * M_DK:M_HEADS * M_DK + (h + 1) * M_DK] * (M_DK ** -0.5)
        v = mv_ref[:, hs]
        qb, kb = q.astype(BF16), k.astype(BF16)
        a_row = li_r[h:h + 1, :] - b_r[M_HEADS + h:M_HEADS + h + 1, :]
        b_col = b_c[:, M_HEADS + h:M_HEADS + h + 1]
        a_col = li_c[:, h:h + 1] - b_col
        m_prev = m_s[h]
        c_prev = c_s[h]
        n_prev = n_s[h]
        a_mat = jnp.where(tril, jnp.broadcast_to(a_row, (L, L)), -jnp.inf)
        g_col = jnp.maximum(m_prev, jnp.max(a_mat, axis=1, keepdims=True))
        d_mat = jnp.exp(a_mat - g_col)
        w_inter = jnp.exp(m_prev - g_col)
        s_mat = lax.dot_general(qb, kb, (((1,), (1,)), ((), ())), preferred_element_type=F32) * d_mat
        cq = lax.dot_general(qb, c_prev.astype(BF16), (((1,), (1,)), ((), ())), preferred_element_type=F32)
        num = jnp.dot(s_mat.astype(BF16), v.astype(BF16), preferred_element_type=F32) + w_inter * cq
        den = jnp.sum(s_mat, axis=1, keepdims=True) + w_inter * jnp.sum(q * n_prev, axis=1, keepdims=True)
        m_t = b_col + g_col
        hh = num / jnp.maximum(jnp.abs(den), jnp.exp(-m_t))
        g_last = g_col[L - 1:L, :]
        w_col = jnp.exp(a_col - g_last)
        decay = jnp.exp(m_prev - g_last)
        vw = (v * w_col).astype(BF16)
        c_s[h] = decay * c_prev + lax.dot_general(vw, kb, (((0,), (0,)), ((), ())), preferred_element_type=F32)
        n_s[h] = decay * n_prev + jnp.sum(k * w_col, axis=0, keepdims=True)
        m_s[h] = b_col[L - 1:L, :] + g_last
        hn = hh * lax.rsqrt(jnp.mean(hh * hh, axis=-1, keepdims=True) + EPS)
        y_ref[:, hs] = (hn * ng_ref[:, hs] * _sigmoid(mo_ref[:, hs])).astype(y_ref.dtype)

    @pl.when(c_id == pl.num_programs(1) - 1)
    def _():
        co_ref[...] = c_s[...]
        no_ref[...] = n_s[...]
        mo_out_ref[...] = m_s[...]


def _mlstm(z, conv_init, c0, n0, m0, prm, nb, nc, n_valid):
    L = CHUNK
    rows = nb * nc * L
    bsel = (lambda b: b) if c0.shape[0] == nb else (lambda b: 0)
    csel = (lambda b: b) if conv_init.shape[0] == nb else (lambda b: 0)
    row = lambda b, c: b * nc + c
    const = lambda shape: pl.BlockSpec(shape, lambda b, c: (0,) * len(shape))
    in_specs = [
        pl.BlockSpec((L, QK_M), lambda b, c: (row(b, c), 0)),
        pl.BlockSpec((L, M_WIDTH), lambda b, c: (row(b, c), Z_MV_BLK)),
        pl.BlockSpec((L, M_WIDTH), lambda b, c: (row(b, c), Z_MO_BLK)),
        pl.BlockSpec((L, LANES), lambda b, c: (row(b, c), Z_GATE_BLK)),
        pl.BlockSpec((None, SUBLANES, QK_M), lambda b, c: (csel(b), 0, 0)),
        pl.BlockSpec((None, M_HEADS, M_DV, M_DK), lambda b, c: (bsel(b), 0, 0, 0)),
        pl.BlockSpec((None, M_HEADS, 1, M_DK), lambda b, c: (bsel(b), 0, 0, 0)),
        pl.BlockSpec((None, M_HEADS, 1, 1), lambda b, c: (bsel(b), 0, 0, 0)),
        const((CONV_W, QK_M)), const((1, QK_M)), const((1, LANES)), const((SUBLANES, L)), const((1, M_WIDTH)),
    ]
    out_specs = (
        pl.BlockSpec((L, M_WIDTH), lambda b, c: (row(b, c), 0)),
        pl.BlockSpec((None, M_HEADS, M_DV, M_DK), lambda b, c: (b, 0, 0, 0)),
        pl.BlockSpec((None, M_HEADS, 1, M_DK), lambda b, c: (b, 0, 0, 0)),
        pl.BlockSpec((None, M_HEADS, 1, 1), lambda b, c: (b, 0, 0, 0)),
    )
    out_shape = (
        jax.ShapeDtypeStruct((rows, M_WIDTH), BF16),
        jax.ShapeDtypeStruct((nb, M_HEADS, M_DV, M_DK), F32),
        jax.ShapeDtypeStruct((nb, M_HEADS, 1, M_DK), F32),
        jax.ShapeDtypeStruct((nb, M_HEADS, 1, 1), F32),
    )
    return pl.pallas_call(
        functools.partial(_mlstm_kernel, n_valid),
        out_shape=out_shape,
        grid=(nb, nc),
        in_specs=in_specs,
        out_specs=out_specs,
        scratch_shapes=[pltpu.VMEM((L + SUBLANES, QK_M), F32), pltpu.VMEM((M_HEADS, M_DV, M_DK), F32),
                        pltpu.VMEM((M_HEADS, 1, M_DK), F32), pltpu.VMEM((M_HEADS, 1, 1), F32)],
        compiler_params=_cparams(("parallel", "arbitrary")),
        name="mlstm",
    )(z, z, z, z, conv_init, c0, n0, m0,
      prm["conv_w"], prm["conv_b"], prm["b_if_row"], prm["b_if_col"], prm["mlstm_g"])


def _mstep_kernel(u_ref, mv_ref, mo_ref, g_ref, sc_ref, c0_ref, n0_ref, m0_ref,
                  cw_ref, cb_ref, bif_ref, ng_ref,
                  y_ref, co_ref, no_ref, mo_out_ref, conv_ref):
    u = u_ref[...]
    sc = sc_ref[...]
    y = sc[0:1, :] * cw_ref[0:1, :] + cb_ref[...]
    y = y + sc[1:2, :] * cw_ref[1:2, :]
    y = y + sc[2:3, :] * cw_ref[2:3, :]
    y = y + u * cw_ref[3:4, :]
    qk = y * _sigmoid(y)
    conv_ref[0:2, :] = sc[1:3, :]
    conv_ref[2:3, :] = u
    gates = g_ref[...] + bif_ref[...]
    lf_all = _log_sigmoid(gates)
    e_i = lax.broadcasted_iota(jnp.int32, (M_DV, M_DV), 0)
    e_j = lax.broadcasted_iota(jnp.int32, (M_DV, M_DV), 1)
    eye = e_i == e_j
    for h in range(M_HEADS):
        hs = slice(h * M_DK, (h + 1) * M_DK)
        q = qk[:, hs]
        k = qk[:, M_HEADS * M_DK + h * M_DK:M_HEADS * M_DK + (h + 1) * M_DK] * (M_DK ** -0.5)
        v = mv_ref[:, hs]
        li = gates[:, h:h + 1]
        lf = lf_all[:, M_HEADS + h:M_HEADS + h + 1]
        m_prev = m0_ref[h]
        c_prev = c0_ref[h]
        n_prev = n0_ref[h]
        m_t = jnp.maximum(lf + m_prev, li)
        d_w = jnp.exp(li - m_t)
        w_inter = jnp.exp(lf + m_prev - m_t)
        s = jnp.sum(q * k, axis=1, keepdims=True) * d_w
        q8 = jnp.broadcast_to(q, (SUBLANES, M_DK)).astype(BF16)
        cq = lax.dot_general(q8, c_prev.astype(BF16), (((1,), (1,)), ((), ())), preferred_element_type=F32)[0:1, :]
        num = s * v + w_inter * cq
        den = s + w_inter * jnp.sum(q * n_prev, axis=1, keepdims=True)
        hh = num / jnp.maximum(jnp.abs(den), jnp.exp(-m_t))
        vdiag = jnp.where(eye, jnp.broadcast_to(v * d_w, (M_DV, M_DV)), 0.0).astype(BF16)
        krows = jnp.broadcast_to(k, (M_DV, M_DK)).astype(BF16)
        co_ref[h] = w_inter * c_prev + jnp.dot(vdiag, krows, preferred_element_type=F32)
        no_ref[h] = w_inter * n_prev + d_w * k
        mo_out_ref[h] = m_t
        hn = hh * lax.rsqrt(jnp.mean(hh * hh, axis=-1, keepdims=True) + EPS)
        y_ref[:, hs] = (hn * ng_ref[:, hs] * _sigmoid(mo_ref[:, hs])).astype(y_ref.dtype)


def _mstep(z3, state_conv, c0, n0, m0, prm):
    nb = z3.shape[0]
    const = lambda shape: pl.BlockSpec(shape, lambda b: (0,) * len(shape))
    zspec = lambda w, blk: pl.BlockSpec((None, 1, w), lambda b, blk=blk: (b, 0, blk))
    st = lambda shape: pl.BlockSpec((None,) + shape, lambda b: (b,) + (0,) * len(shape))
    return pl.pallas_call(
        _mstep_kernel,
        out_shape=(jax.ShapeDtypeStruct((nb, 1, M_WIDTH), BF16),
                   jax.ShapeDtypeStruct((nb, M_HEADS, M_DV, M_DK), F32),
                   jax.ShapeDtypeStruct((nb, M_HEADS, 1, M_DK), F32),
                   jax.ShapeDtypeStruct((nb, M_HEADS, 1, 1), F32),
                   jax.ShapeDtypeStruct((nb, CONV_W - 1, QK_M), F32)),
        grid=(nb,),
        in_specs=[zspec(QK_M, 0), zspec(M_WIDTH, Z_MV_BLK), zspec(M_WIDTH, Z_MO_BLK), zspec(LANES, Z_GATE_BLK),
                  st((CONV_W - 1, QK_M)), st((M_HEADS, M_DV, M_DK)), st((M_HEADS, 1, M_DK)), st((M_HEADS, 1, 1)),
                  const((CONV_W, QK_M)), const((1, QK_M)), const((1, LANES)), const((1, M_WIDTH))],
        out_specs=(pl.BlockSpec((None, 1, M_WIDTH), lambda b: (b, 0, 0)),
                   st((M_HEADS, M_DV, M_DK)), st((M_HEADS, 1, M_DK)), st((M_HEADS, 1, 1)), st((CONV_W - 1, QK_M))),
        compiler_params=_cparams(("parallel",)),
        name="mstep",
    )(z3, z3, z3, z3, state_conv, c0, n0, m0,
      prm["conv_w"], prm["conv_b"], prm["b_if_row"], prm["mlstm_g"])


def _diff_lambda(l_ref):
    a = jnp.sum(l_ref[0:1, :] * l_ref[1:2, :], axis=1, keepdims=True)
    b = jnp.sum(l_ref[2:3, :] * l_ref[3:4, :], axis=1, keepdims=True)
    return jnp.exp(a) - jnp.exp(b) + LAM_INIT


def _class_allreduce(x, op):
    for sh in (A_HEADS, 2 * A_HEADS, 4 * A_HEADS, 8 * A_HEADS):
        x = op(x, pltpu.roll(x, sh, axis=1))
    return x


def _decode_update(kmat, vmat, n_lanes_valid, m_s, l_s, acc_s):
    nt = (((1,), (1,)), ((), ()))
    r_i = lax.broadcasted_iota(jnp.int32, (SUBLANES, LANES), 0)
    l_i = lax.broadcasted_iota(jnp.int32, (SUBLANES, LANES), 1)
    rsel = jnp.where((r_i < 2) & ((l_i >= A_DH) == (r_i == 1)), 1.0, 0.0).astype(BF16)
    diag = (l_i % A_HEADS) == r_i
    rows = kmat.shape[0]
    st = lax.dot_general(rsel, kmat.astype(BF16), nt, preferred_element_type=F32)
    tiles = [st[:, t * LANES:(t + 1) * LANES] for t in range(rows // LANES)]
    if n_lanes_valid is not None:
        tiles = [jnp.where(l_i + t * LANES < n_lanes_valid, x, -jnp.inf) for t, x in enumerate(tiles)]
    mb = tiles[0]
    for x in tiles[1:]:
        mb = jnp.maximum(mb, x)
    mb = _class_allreduce(mb, jnp.maximum)
    m_old = m_s[...]
    m_new = jnp.maximum(m_old, mb)
    alpha = jnp.exp2(m_old - m_new)
    ps = [jnp.exp2(x - m_new) for x in tiles]
    lsum = ps[0]
    for x in ps[1:]:
        lsum = lsum + x
    l_s[...] = alpha * l_s[...] + lsum
    m_s[...] = m_new
    pm = []
    for c in range(2):
        pm.append(jnp.concatenate(
            [jnp.where(diag, jnp.broadcast_to(x[c:c + 1, :], (SUBLANES, LANES)), 0.0) for x in ps], axis=1))
    pm = jnp.concatenate(pm, axis=0).astype(BF16)
    pv = jnp.dot(pm, vmat.astype(BF16), preferred_element_type=F32)
    acc_s[...] = _decode_colvec(alpha) * acc_s[...] + pv


def _decode_colvec(x_rep):
    r_i = lax.broadcasted_iota(jnp.int32, (SUBLANES, LANES), 0)
    l_i = lax.broadcasted_iota(jnp.int32, (SUBLANES, LANES), 1)
    cols = []
    for c in range(2):
        xb = jnp.broadcast_to(x_rep[c:c + 1, :], (SUBLANES, LANES))
        cols.append(jnp.sum(jnp.where(l_i == r_i, xb, 0.0), axis=1, keepdims=True))
    return jnp.concatenate(cols, axis=0)


def _attn_decode_kernel(n_meta, rb, nq, g_pages, n_dec, dps, spp,
                        pt_ref, q_ref, k_ref, v_ref, km_ref, vm_ref, lam_ref, ag_ref, qd_ref, kn_ref, vn_ref,
                        ck_ref, cv_ref, o_ref, od_ref,
                        qq_s, mx_s, acc_s, s_s, sd_s, kbuf, vbuf, sem, dm_s, dl_s, dacc_s):
    tq = q_ref.shape[0]
    b, h, i = pl.program_id(0), pl.program_id(1), pl.program_id(2)
    step = (b * pl.num_programs(1) + h) * pl.num_programs(2) + i

    def page_copies(d):
        slot, sb, pg = lax.rem(d, 2), lax.div(d, spp), lax.rem(d, spp)
        cps = []
        for t in range(g_pages):
            pidx = pt_ref[sb, pg * g_pages + t]
            cps.append(pltpu.make_async_copy(ck_ref.at[pidx], kbuf.at[slot, t], sem.at[0, slot]))
            cps.append(pltpu.make_async_copy(cv_ref.at[pidx], vbuf.at[slot, t], sem.at[1, slot]))
        return cps

    def issue(d):
        @pl.when(d < n_dec)
        def _():
            for cp in page_copies(d):
                cp.start()

    def decode_step(d):
        @pl.when(d < n_dec)
        def _():
            for cp in page_copies(d):
                cp.wait()
            slot, sb, pg = lax.rem(d, 2), lax.div(d, spp), lax.rem(d, spp)
            q = qd_ref[sb]

            @pl.when(pg == 0)
            def _():
                dm_s[...] = jnp.full(dm_s.shape, NEG_BIG, F32)
                dl_s[...] = jnp.zeros(dl_s.shape, F32)
                dacc_s[...] = jnp.zeros(dacc_s.shape, F32)
                pad = jnp.zeros((LANES - SUBLANES, LANES), F32)
                _decode_update(jnp.concatenate([kn_ref[sb] * q, pad], axis=0),
                               jnp.concatenate([vn_ref[sb], pad], axis=0), A_HEADS, dm_s, dl_s, dacc_s)

            kmat = jnp.concatenate([(kbuf[slot, t, :, 0] * q[None, :, :]).reshape(-1, LANES) for t in range(g_pages)],
                                   axis=0)
            vmat = jnp.concatenate([vbuf[slot, t, :, 0].reshape(-1, LANES) for t in range(g_pages)], axis=0)
            _decode_update(kmat, vmat, None, dm_s, dl_s, dacc_s)

            @pl.when(pg == spp - 1)
            def _():
                lam = _diff_lambda(lam_ref)
                l_rep = _class_allreduce(dl_s[...], jnp.add)
                o = dacc_s[...] / _decode_colvec(l_rep)
                o = o[0:A_HEADS, :] - lam * o[A_HEADS:2 * A_HEADS, :]
                on = o * lax.rsqrt(jnp.mean(o * o, axis=-1, keepdims=True) + EPS)
                od_ref[sb] = (on * ag_ref[...] * (1.0 - LAM_INIT)).astype(od_ref.dtype)

    d0 = step * dps

    @pl.when(step == 0)
    def _():
        issue(0)

    issue(d0 + 1)

    q = q_ref[...]
    lane = lax.broadcasted_iota(jnp.int32, q.shape, 1)
    zero = jnp.zeros_like(q)
    qq_s[0:tq, :] = jnp.where(lane < A_DH, q, zero)
    qq_s[tq:2 * tq, :] = jnp.where(lane >= A_DH, q, zero)
    nt = (((1,), (1,)), ((), ()))
    blocks = [(r0, r0 % tq) for r0 in range(0, 2 * tq, rb)]

    def scores(r0, kc):
        return lax.dot_general(qq_s[r0:r0 + rb, :], kc, nt, preferred_element_type=F32)

    def fold_max(m, s):
        for t in range(s.shape[1] // LANES):
            m = jnp.maximum(m, s[:, t * LANES:(t + 1) * LANES])
        return m

    def probs(s, m_rep):
        return jnp.exp2(s - jnp.concatenate([m_rep] * (s.shape[1] // LANES), axis=1)).astype(BF16)

    def with_ones(v):
        return jnp.concatenate([v, jnp.ones((v.shape[0], LANES), v.dtype)], axis=1)

    def pass1_chunk(j, width):
        kc = k_ref[pl.ds(pl.multiple_of(j * tq, tq), width), :]
        for r0, _ in blocks:
            s = scores(r0, kc)
            for w in range(width // tq):
                s_s[j + w, r0:r0 + rb, :] = s[:, w * tq:(w + 1) * tq]
            mx_s[r0:r0 + rb, :] = fold_max(mx_s[r0:r0 + rb, :], s)

    def pass2_chunk(j, width):
        v1 = with_ones(v_ref[pl.ds(pl.multiple_of(j * tq, tq), width), :])
        for r0, _ in blocks:
            s = jnp.concatenate([s_s[j + w, r0:r0 + rb, :] for w in range(width // tq)], axis=1)
            acc_s[r0:r0 + rb, :] += jnp.dot(probs(s, mx_s[r0:r0 + rb, :]), v1, preferred_element_type=F32)

    def full_chunks(chunk_fn):
        if nq >= 2:
            def pair(jj, carry):
                chunk_fn(2 * jj, 2 * tq)
                return carry

            lax.fori_loop(0, lax.shift_right_logical(i, 1), pair, 0)

            @pl.when(lax.rem(i, 2) == 1)
            def _():
                chunk_fn(i - 1, tq)

    dstart = pl.multiple_of(i * tq, tq)

    for r0, q0 in blocks:
        n = q0 + rb
        s = scores(r0, jnp.concatenate([km_ref[...], k_ref[pl.ds(dstart, n), :]], axis=0))
        row = lax.broadcasted_iota(jnp.int32, s.shape, 0) + q0
        col = lax.broadcasted_iota(jnp.int32, s.shape, 1)
        s = jnp.where((col < n_meta) | ((col >= LANES) & (col - LANES <= row)), s, -jnp.inf)
        sd_s[r0:r0 + rb, 0:LANES + n] = s
        mx_s[r0:r0 + rb, :] = fold_max(s[:, 0:LANES], s[:, LANES:])
    full_chunks(pass1_chunk)
    for r0, _ in blocks:
        m = jnp.max(mx_s[r0:r0 + rb, :], axis=1, keepdims=True)
        mx_s[r0:r0 + rb, :] = jnp.broadcast_to(m, (rb, LANES))

    for r0, q0 in blocks:
        n = q0 + rb
        p = probs(sd_s[r0:r0 + rb, 0:LANES + n], mx_s[r0:r0 + rb, :])
        v1 = with_ones(jnp.concatenate([vm_ref[...], v_ref[pl.ds(dstart, n), :]], axis=0))
        acc_s[r0:r0 + rb, :] = jnp.dot(p, v1, preferred_element_type=F32)
    full_chunks(pass2_chunk)

    o0 = acc_s[0:tq, 0:A_DV] / acc_s[0:tq, A_DV:2 * A_DV]
    o1 = acc_s[tq:2 * tq, 0:A_DV] / acc_s[tq:2 * tq, A_DV:2 * A_DV]
    o = o0 - _diff_lambda(lam_ref) * o1
    on = o * lax.rsqrt(jnp.mean(o * o, axis=-1, keepdims=True) + EPS)
    o_ref[...] = (on * ag_ref[...] * (1.0 - LAM_INIT)).astype(o_ref.dtype)

    decode_step(d0)
    for u in range(1, dps):
        issue(d0 + u + 1)
        decode_step(d0 + u)


def _attn_decode(q, k, v, k_meta, v_meta, n_meta, qd, k_new, v_new, cache_k, cache_v, page_table, lam4, attn_g):
    bsz, t, _ = q.shape
    db, n_pages = page_table.shape
    page = cache_k.shape[1]
    tq = _pick(t, 512)
    rb = _pick(tq, 256)
    nq = t // tq
    g_pages = 8 if n_pages % 8 == 0 else (2 if n_pages % 2 == 0 else 1)
    spp = n_pages // g_pages
    n_dec = db * spp
    n_steps = bsz * A_HEADS * nq
    dps = -(-n_dec // n_steps)
    hspec = lambda rows: pl.BlockSpec((None, rows, LANES), lambda b, h, i, pt: (b, 0, h))
    const = lambda shape: pl.BlockSpec(shape, lambda b, h, i, pt: (0,) * len(shape))
    qblk = pl.BlockSpec((None, tq, LANES), lambda b, h, i, pt: (b, i, h))
    grid_spec = pltpu.PrefetchScalarGridSpec(
        num_scalar_prefetch=1,
        grid=(bsz, A_HEADS, nq),
        in_specs=[qblk, hspec(t), hspec(t),
                  pl.BlockSpec((LANES, LANES), lambda b, h, i, pt: (0, h)),
                  pl.BlockSpec((LANES, LANES), lambda b, h, i, pt: (0, h)),
                  const((4, A_DH)), const((1, A_DV)),
                  const((db, A_HEADS, A_DV)), const((db, A_HEADS, A_DV)), const((db, A_HEADS, A_DV)),
                  pl.BlockSpec(memory_space=pl.ANY), pl.BlockSpec(memory_space=pl.ANY)],
        out_specs=(qblk, const((db, A_HEADS, A_DV))),
        scratch_shapes=[pltpu.VMEM((2 * tq, LANES), BF16), pltpu.VMEM((2 * tq, LANES), F32),
                        pltpu.VMEM((2 * tq, 2 * A_DV), F32), pltpu.VMEM((max(nq - 1, 1), 2 * tq, tq), F32),
                        pltpu.VMEM((2 * tq, LANES + tq), F32),
                        pltpu.VMEM((2, g_pages, page, 1, A_HEADS, A_DV), F32),
                        pltpu.VMEM((2, g_pages, page, 1, A_HEADS, A_DV), F32),
                        pltpu.SemaphoreType.DMA((2, 2)),
                        pltpu.VMEM((SUBLANES, LANES), F32), pltpu.VMEM((SUBLANES, LANES), F32),
                        pltpu.VMEM((2 * A_HEADS, A_DV), F32)],
    )
    return pl.pallas_call(
        functools.partial(_attn_decode_kernel, n_meta, rb, nq, g_pages, n_dec, dps, spp),
        out_shape=(jax.ShapeDtypeStruct((bsz, t, A_WIDTH), BF16), jax.ShapeDtypeStruct((db, A_HEADS, A_DV), BF16)),
        grid_spec=grid_spec,
        compiler_params=_cparams(("arbitrary", "arbitrary", "arbitrary")),
        name="attn_decode",
    )(page_table, q, k, v, k_meta, v_meta, lam4, attn_g, qd, k_new, v_new, cache_k, cache_v)


def _post_kernel(x_ref, ym_ref, ya_ref, ga_ref, gb_ref, wpa_ref, wpb_ref, wo_ref, g1_ref, wu_ref, wd_ref, g2_ref, o_ref):
    pa = jnp.dot(ym_ref[...], wpa_ref[...], preferred_element_type=F32)
    pb = jnp.dot(ya_ref[...], wpb_ref[...], preferred_element_type=F32)
    mixed = _sigmoid(ga_ref[...]) * pa + _sigmoid(gb_ref[...]) * pb
    x = x_ref[...] + jnp.dot(mixed.astype(BF16), wo_ref[...], preferred_element_type=F32)
    xn = (x * lax.rsqrt(jnp.mean(x * x, axis=-1, keepdims=True) + EPS) * g1_ref[...]).astype(BF16)
    acc = x
    fc = D_MODEL
    for f in range(D_FF // fc):
        hf = jnp.dot(xn, wu_ref[:, f * fc:(f + 1) * fc], preferred_element_type=F32)
        hf = jnp.square(jnp.maximum(hf, 0.0)).astype(BF16)
        acc = acc + jnp.dot(hf, wd_ref[f * fc:(f + 1) * fc, :], preferred_element_type=F32)
    o_ref[...] = acc * lax.rsqrt(jnp.mean(acc * acc, axis=-1, keepdims=True) + EPS) * g2_ref[...]


def _post(x, y_m, y_a, z, prm):
    m = x.shape[0]
    tm = _pick(m, 512)
    rspec = lambda w, blk=0: pl.BlockSpec((tm, w), lambda i, blk=blk: (i, blk))
    const = lambda shape: pl.BlockSpec(shape, lambda i: (0, 0), pipeline_mode=pl.Buffered(1))
    return pl.pallas_call(
        _post_kernel,
        out_shape=jax.ShapeDtypeStruct((m, D_MODEL), F32),
        grid=(m // tm,),
        in_specs=[rspec(D_MODEL), rspec(M_WIDTH), rspec(A_WIDTH), rspec(D_MODEL, Z_GA_BLK), rspec(D_MODEL, Z_GB_BLK),
                  const((M_WIDTH, D_MODEL)), const((A_WIDTH, D_MODEL)), const((D_MODEL, D_MODEL)),
                  const((1, D_MODEL)), const((D_MODEL, D_FF)), const((D_FF, D_MODEL)), const((1, D_MODEL))],
        out_specs=rspec(D_MODEL),
        compiler_params=_cparams(("parallel",)),
        name="post",
    )(x, y_m, y_a, z, z, prm["w_pa"], prm["w_pb"], prm["w_out"], prm["ffn_g"], prm["w_up"], prm["w_down"], prm["final_g"])


def kernel(x_prompt, x_sample, cache_k, cache_v, state_C, state_n, state_m, state_conv, page_table, meta_tokens, norm_mix_g, norm_ffn_g, w_in, b_if, conv_w, conv_b, mlstm_norm_g, lambda_q1, lambda_k1, lambda_q2, lambda_k2, attn_norm_g, w_proj_a, w_proj_b, w_out, w_up, w_down, final_norm_g):
    bsz, seq, d = x_prompt.shape
    dbsz, s_dec, _ = x_sample.shape
    n_pages = page_table.shape[1]
    past = n_pages * cache_k.shape[1]
    assert d == D_MODEL and s_dec == 1 and seq % CHUNK == 0 and norm_mix_g.shape[0] == 1
    assert N_META + dbsz <= CHUNK
    l = 0

    gate_lo = QK_M + 2 * M_WIDTH
    gate_hi = gate_lo + 2 * M_HEADS
    w = w_in[l]
    w_re = jnp.concatenate([w[:, :gate_lo], w[:, gate_hi:], w[:, gate_lo:gate_hi],
                            jnp.zeros((d, LANES - 2 * M_HEADS), w.dtype)], axis=1).astype(BF16)
    bif = b_if[l].astype(F32)
    prm = {
        "conv_w": conv_w[l].astype(F32), "conv_b": conv_b[l].astype(F32)[None, :],
        "b_if_row": jnp.pad(bif, (0, LANES - 2 * M_HEADS))[None, :],
        "b_if_col": jnp.broadcast_to(bif[:, None], (2 * M_HEADS, CHUNK)),
        "mlstm_g": mlstm_norm_g[l].astype(F32)[None, :],
        "w_pa": w_proj_a[l].astype(BF16), "w_pb": w_proj_b[l].astype(BF16), "w_out": w_out[l].astype(BF16),
        "ffn_g": norm_ffn_g[l].astype(F32)[None, :], "w_up": w_up[l].astype(BF16), "w_down": w_down[l].astype(BF16),
        "final_g": final_norm_g.astype(F32)[None, :],
    }
    mix_g = norm_mix_g[l].astype(F32)[None, :]
    lam4 = jnp.stack([lambda_q1[l], lambda_k1[l], lambda_q2[l], lambda_k2[l]]).astype(F32)
    attn_g = attn_norm_g[l].astype(F32)[None, :]

    xp = x_prompt.reshape(bsz * seq, d)
    xs = x_sample.reshape(dbsz, d)
    xe = jnp.concatenate([meta_tokens.astype(F32), xs, jnp.zeros((CHUNK - N_META - dbsz, d), F32)], axis=0)
    z_p = _proj(xp, mix_g, w_re)
    z_e = _proj(xe, mix_g, w_re)

    pos_e = np.concatenate([np.arange(N_META), np.full((dbsz,), past), np.zeros((CHUNK - N_META - dbsz,), np.int64)])
    q_e, kb_e, vb_e, k_e, v_e = _rope(z_e, _rope_tables(pos_e), CHUNK)
    heads = lambda a: a.reshape(a.shape[0], A_HEADS, A_DV)
    q_p, kb_p, vb_p, k_p, v_p = _rope(z_p, _rope_tables(N_META + np.arange(seq)), seq,
                                      (heads(k_e[:N_META]), heads(v_e[:N_META])))

    zero_state = (jnp.zeros((1, M_HEADS, M_DV, M_DK), F32), jnp.zeros((1, M_HEADS, 1, M_DK), F32),
                  jnp.zeros((1, M_HEADS, 1, 1), F32))
    _, c_m, n_m, m_m = _mlstm(z_e, jnp.zeros((1, SUBLANES, QK_M), F32), *zero_state, prm, 1, 1, N_META)
    conv_init = z_e[N_META - SUBLANES:N_META, :QK_M][None]
    ym_p, c_p, n_p, m_p = _mlstm(z_p, conv_init, c_m, n_m, m_m, prm, bsz, seq // CHUNK, CHUNK)
    z_s = z_e[N_META:N_META + dbsz]
    ym_s, c_s, n_s, m_s, conv_s = _mstep(
        z_s[:, None, :], state_conv[l].astype(F32), state_C[l].astype(F32),
        state_n[l].astype(F32)[:, :, None, :], state_m[l].astype(F32)[:, :, None, None], prm)

    hv = lambda a: a[N_META:N_META + dbsz].reshape(dbsz, A_HEADS, A_DV)
    v_s = v_e[N_META:N_META + dbsz]
    ya_p, ya_s = _attn_decode(
        q_p.reshape(bsz, seq, A_QK), kb_p.reshape(bsz, seq, A_QK), vb_p.reshape(bsz, seq, A_WIDTH), kb_e, vb_e, N_META,
        hv(q_e).astype(F32), hv(k_e), v_s.reshape(dbsz, A_HEADS, A_DV), cache_k, cache_v,
        page_table.astype(jnp.int32), lam4, attn_g)

    y_p = _post(xp, ym_p, ya_p.reshape(bsz * seq, A_WIDTH), z_p, prm)
    y_s = _post(xs, ym_s.reshape(dbsz, M_WIDTH), ya_s.reshape(dbsz, A_WIDTH), z_s, prm)

    k_prompt = k_p.reshape(bsz, N_META + seq, 1, A_HEADS, A_DV)
    v_prompt = v_p.reshape(bsz, N_META + seq, 1, A_HEADS, A_DV)
    conv_prompt = z_p.reshape(bsz, seq, Z_COLS)[:, seq - (CONV_W - 1):, :QK_M][None]
    return (y_p.reshape(bsz, seq, d), y_s.reshape(dbsz, 1, d),
            k_prompt, v_prompt,
            c_p[None], n_p.reshape(1, bsz, M_HEADS, M_DK), m_p.reshape(1, bsz, M_HEADS), conv_prompt,
            k_e[N_META:N_META + dbsz].reshape(dbsz, 1, 1, A_HEADS, A_DV), v_s.reshape(dbsz, 1, 1, A_HEADS, A_DV),
            c_s[None], n_s.reshape(1, dbsz, M_HEADS, M_DK), m_s.reshape(1, dbsz, M_HEADS), conv_s[None])
```
